```python
import math
import jax, jax.numpy as jnp
from jax import lax
import numpy as np

D_MODEL = 1024
BATCH = 8
SEQ = 4096
DEPTH = 1

HEAD_DIM = 64
V_HEAD_DIM = 2 * HEAD_DIM
N_HEADS = D_MODEL // V_HEAD_DIM
ATTN_WIDTH = N_HEADS * V_HEAD_DIM
QK_COLS = N_HEADS * 2 * HEAD_DIM
CONV_CH = D_MODEL
CONV_WIDTH = 31
D_FF = 4 * D_MODEL
NUM_BUCKETS = 32
MAX_DISTANCE = 128
Q_BLOCK = 128
IN_COLS = 2 * QK_COLS + ATTN_WIDTH + 2 * CONV_CH + 2 * D_MODEL
EPS = 1e-6

kernel_name = "hybrid_diffattn_conformer_gated"


def rmsnorm(x, g):
    xf = x.astype(jnp.float32)
    y = xf * lax.rsqrt(jnp.mean(xf * xf, axis=-1, keepdims=True) + EPS)
    return y.astype(x.dtype) * g


def layernorm(x, g, b):
    xf = x.astype(jnp.float32)
    mu = jnp.mean(xf, axis=-1, keepdims=True)
    var = jnp.mean(jnp.square(xf - mu), axis=-1, keepdims=True)
    y = (xf - mu) * lax.rsqrt(var + EPS)
    return y.astype(x.dtype) * g + b


def t5_causal_bucket(rel):
    n = jnp.maximum(rel, 0)
    max_exact = NUM_BUCKETS // 2
    nf = jnp.maximum(n, 1).astype(jnp.float32)
    large = max_exact + (jnp.log(nf / max_exact) / math.log(MAX_DISTANCE / max_exact)
                         * (NUM_BUCKETS - max_exact)).astype(jnp.int32)
    large = jnp.minimum(large, NUM_BUCKETS - 1)
    return jnp.where(n < max_exact, n, large)


def diff_attention(q1, q2, k1, k2, v, rel_bias, lam):
    B, S, H, d = q1.shape
    nb = S // Q_BLOCK
    scale = 1.0 / math.sqrt(d)
    def to_blocks(t):
        return t.reshape(B, nb, Q_BLOCK, H, d).transpose(1, 0, 2, 3, 4)
    q1b, q2b = to_blocks(q1), to_blocks(q2)
    kpos = jnp.arange(S, dtype=jnp.int32)

    def block(args):
        idx, q1_blk, q2_blk = args
        qpos = idx * Q_BLOCK + jnp.arange(Q_BLOCK, dtype=jnp.int32)
        rel = qpos[:, None] - kpos[None, :]
        bias = jnp.transpose(rel_bias[t5_causal_bucket(rel)], (2, 0, 1)).astype(jnp.float32)
        visible = (rel >= 0)[None, None]
        s1 = jnp.einsum('bqhd,bkhd->bhqk', q1_blk, k1).astype(jnp.float32) * scale + bias
        s2 = jnp.einsum('bqhd,bkhd->bhqk', q2_blk, k2).astype(jnp.float32) * scale + bias
        p1 = jax.nn.softmax(jnp.where(visible, s1, -jnp.inf), axis=-1)
        p2 = jax.nn.softmax(jnp.where(visible, s2, -jnp.inf), axis=-1)
        a = (p1 - lam * p2).astype(v.dtype)
        return jnp.einsum('bhqk,bkhe->bqhe', a, v)

    out = lax.map(block, (jnp.arange(nb, dtype=jnp.int32), q1b, q2b))
    return out.transpose(1, 0, 2, 3, 4).reshape(B, S, H, V_HEAD_DIM)


def causal_depthwise_conv(x, w, b):
    C = x.shape[-1]
    y = lax.conv_general_dilated(
        x, w.reshape(CONV_WIDTH, 1, C).astype(x.dtype),
        window_strides=(1,), padding=((CONV_WIDTH - 1, 0),),
        dimension_numbers=('NWC', 'WIO', 'NWC'), feature_group_count=C)
    return y + b


def hybrid_mixer(u, w_in, b_glu, b_gate, lam_q1, lam_k1, lam_q2, lam_k2, subln_g,
                 conv_w, conv_b, conv_ln_g, conv_ln_b, w_pw2, b_pw2, w_out,
                 rel_bias, lambda_init):
    B, S, _ = u.shape
    proj = u @ w_in
    c0 = QK_COLS
    c1 = 2 * QK_COLS
    c2 = c1 + ATTN_WIDTH
    c3 = c2 + 2 * CONV_CH
    q, k, v, glu_in, gate_in = jnp.split(proj, [c0, c1, c2, c3], axis=-1)

    q = q.reshape(B, S, N_HEADS, 2, HEAD_DIM)
    k = k.reshape(B, S, N_HEADS, 2, HEAD_DIM)
    v = v.reshape(B, S, N_HEADS, V_HEAD_DIM)
    lam = (jnp.exp(jnp.sum(lam_q1.astype(jnp.float32) * lam_k1.astype(jnp.float32)))
           - jnp.exp(jnp.sum(lam_q2.astype(jnp.float32) * lam_k2.astype(jnp.float32)))
           + lambda_init)
    attn = diff_attention(q[..., 0, :], q[..., 1, :], k[..., 0, :], k[..., 1, :], v, rel_bias, lam)
    attn = (rmsnorm(attn, subln_g) * (1.0 - lambda_init)).reshape(B, S, ATTN_WIDTH)

    ga, gb = jnp.split(glu_in + b_glu, 2, axis=-1)
    c = ga * jax.nn.sigmoid(gb)
    c = causal_depthwise_conv(c, conv_w, conv_b)
    c = jax.nn.silu(layernorm(c, conv_ln_g, conv_ln_b))
    conv_out = c @ w_pw2 + b_pw2

    g_attn, g_conv = jnp.split(jax.nn.sigmoid(gate_in + b_gate), 2, axis=-1)
    return (g_attn * attn + g_conv * conv_out) @ w_out


def sq_relu_mlp(u, w_up, w_down):
    return jnp.square(jax.nn.relu(u @ w_up)) @ w_down


def setup_inputs(seed: int = 0) -> dict:
    key = jax.random.key(seed)
    ks = jax.random.split(key, 24)
    f32 = jnp.float32
    def nrm(k, shape, scale):
        return jax.random.normal(k, shape, f32) * scale
    def gain(k, shape):
        return 1.0 + 0.02 * jax.random.normal(k, shape, f32)
    L = DEPTH
    return {
        "x": nrm(ks[0], (BATCH, SEQ, D_MODEL), 1.0),
        "rel_bias": nrm(ks[1], (NUM_BUCKETS, N_HEADS), 0.5),
        "final_norm_g": gain(ks[2], (D_MODEL,)),
        "norm_mix_g": gain(ks[3], (L, D_MODEL)),
        "w_in": nrm(ks[4], (L, D_MODEL, IN_COLS), D_MODEL ** -0.5),
        "b_glu": nrm(ks[5], (L, 2 * CONV_CH), 0.02),
        "b_gate": nrm(ks[6], (L, 2 * D_MODEL), 0.02),
        "lam_q1": nrm(ks[7], (L, HEAD_DIM), 0.1),
        "lam_k1": nrm(ks[8], (L, HEAD_DIM), 0.1),
        "lam_q2": nrm(ks[9], (L, HEAD_DIM), 0.1),
        "lam_k2": nrm(ks[10], (L, HEAD_DIM), 0.1),
        "subln_g": gain(ks[11], (L, V_HEAD_DIM)),
        "conv_w": nrm(ks[12], (L, CONV_WIDTH, CONV_CH), CONV_WIDTH ** -0.5),
        "conv_b": nrm(ks[13], (L, CONV_CH), 0.02),
        "conv_ln_g": gain(ks[14], (L, CONV_CH)),
        "conv_ln_b": nrm(ks[15], (L, CONV_CH), 0.02),
        "w_pw2": nrm(ks[16], (L, CONV_CH, D_MODEL), CONV_CH ** -0.5),
        "b_pw2": nrm(ks[17], (L, D_MODEL), 0.02),
        "w_out": nrm(ks[18], (L, D_MODEL, D_MODEL), D_MODEL ** -0.5),
        "norm_mlp_g": gain(ks[19], (L, D_MODEL)),
        "w_up": nrm(ks[20], (L, D_MODEL, D_FF), D_MODEL ** -0.5),
        "w_down": nrm(ks[21], (L, D_FF, D_MODEL), D_FF ** -0.5),
    }


def reference(x, rel_bias, final_norm_g, norm_mix_g, w_in, b_glu, b_gate,
              lam_q1, lam_k1, lam_q2, lam_k2, subln_g, conv_w, conv_b,
              conv_ln_g, conv_ln_b, w_pw2, b_pw2, w_out, norm_mlp_g, w_up, w_down):
    h = x
    for l in range(DEPTH):
        lambda_init = 0.8 - 0.6 * math.exp(-0.3 * l)
        h = h + hybrid_mixer(rmsnorm(h, norm_mix_g[l]), w_in[l], b_glu[l], b_gate[l],
                             lam_q1[l], lam_k1[l], lam_q2[l], lam_k2[l], subln_g[l],
                             conv_w[l], conv_b[l], conv_ln_g[l], conv_ln_b[l],
                             w_pw2[l], b_pw2[l], w_out[l], rel_bias, lambda_init)
        h = h + sq_relu_mlp(rmsnorm(h, norm_mlp_g[l]), w_up[l], w_down[l])
    return rmsnorm(h, final_norm_g)
```

```python
import functools
import math

import jax
import jax.numpy as jnp
import numpy as np
from jax import lax
from jax.experimental import pallas as pl
from jax.experimental.pallas import tpu as pltpu

HEAD_DIM = 64
V_HEAD_DIM = 2 * HEAD_DIM
CONV_WIDTH = 31
NUM_BUCKETS = 32
MAX_DISTANCE = 128
EPS = 1e-6
MASKED = -1e30

PROJ_ROWS = 512
ATTN_TILE = 512
CONV_ROWS = 256
CONV_HALO = 32
CONV_CHUNK = 32
MLP_ROWS = 512
VMEM_LIMIT = 56 * 1024 * 1024

_BF16 = jnp.bfloat16
_F32 = jnp.float32


def _resident(shape):
    return pl.BlockSpec(shape, lambda *_: (0,) * len(shape), pipeline_mode=pl.Buffered(1))


def _params(*semantics):
    return pltpu.CompilerParams(dimension_semantics=semantics, vmem_limit_bytes=VMEM_LIMIT)


def _rms(xf):
    return xf * lax.rsqrt(jnp.mean(xf * xf, axis=-1, keepdims=True) + EPS)


def _dot(a, b):
    return jnp.dot(a, b, preferred_element_type=_F32)


def _dot_nt(a, b):
    return lax.dot_general(a, b, (((1,), (1,)), ((), ())), preferred_element_type=_F32)


def _proj_kernel(x_ref, g_ref, wqk_ref, wvt_ref, wglu_ref, wgate_ref, bglu_ref, bgate_ref,
                 q_ref, k_ref, vt_ref, c_ref, gate_ref, *, d_qk, kv_tiles_per_block):
    u = (_rms(x_ref[...]) * g_ref[...]).astype(_BF16)
    scale = 1.0 / math.sqrt(HEAD_DIM)
    q_ref[...] = (_dot(u, wqk_ref[:, :d_qk]) * scale).astype(_BF16)
    k_ref[...] = _dot(u, wqk_ref[:, d_qk:]).astype(_BF16)
    vt = _dot_nt(wvt_ref[...], u).astype(_BF16)
    for t in range(kv_tiles_per_block):
        vt_ref[0, t] = vt[:, t * ATTN_TILE:(t + 1) * ATTN_TILE]
    ch = c_ref.shape[-1]
    ga = _dot(u, wglu_ref[:, :ch]) + bglu_ref[:, :ch]
    gb = _dot(u, wglu_ref[:, ch:]) + bglu_ref[:, ch:]
    c_ref[...] = ga * jax.nn.sigmoid(gb)
    gate_ref[...] = jax.nn.sigmoid(_dot(u, wgate_ref[...]) + bgate_ref[...]).astype(_BF16)


def _proj(x2, g, wqk, wvt, wglu, wgate, bglu, bgate, *, batch, seq):
    n, d = x2.shape
    d_qk = wqk.shape[1] // 2
    aw = wvt.shape[0]
    ch = wglu.shape[1] // 2
    ngate = wgate.shape[1]
    rows = PROJ_ROWS
    tiles = rows // ATTN_TILE
    blocks_per_seq = seq // rows
    row_spec = lambda w: pl.BlockSpec((rows, w), lambda i: (i, 0))
    return pl.pallas_call(
        functools.partial(_proj_kernel, d_qk=d_qk, kv_tiles_per_block=tiles),
        grid=(n // rows,),
        in_specs=[row_spec(d), _resident((1, d)), _resident(wqk.shape), _resident(wvt.shape),
                  _resident(wglu.shape), _resident(wgate.shape), _resident(bglu.shape),
                  _resident(bgate.shape)],
        out_specs=[row_spec(d_qk), row_spec(d_qk),
                   pl.BlockSpec((1, tiles, aw, ATTN_TILE),
                                lambda i: (i // blocks_per_seq, i % blocks_per_seq, 0, 0)),
                   row_spec(ch), row_spec(ngate)],
        out_shape=[jax.ShapeDtypeStruct((n, d_qk), _BF16), jax.ShapeDtypeStruct((n, d_qk), _BF16),
                   jax.ShapeDtypeStruct((batch, seq // ATTN_TILE, aw, ATTN_TILE), _BF16),
                   jax.ShapeDtypeStruct((n, ch), _F32), jax.ShapeDtypeStruct((n, ngate), _BF16)],
        compiler_params=_params("parallel"),
        name="proj",
    )(x2, g, wqk, wvt, wglu, wgate, bglu, bgate)


def _bucket_tiles():
    t = ATTN_TILE
    r = np.arange(t, dtype=np.int64)[:, None]
    c = np.arange(t, dtype=np.int64)[None, :]

    def bucket(rel):
        n = np.maximum(rel, 0)
        max_exact = NUM_BUCKETS // 2
        nf = np.maximum(n, 1).astype(np.float32)
        large = max_exact + (np.log(nf / np.float32(max_exact)) / np.float32(math.log(MAX_DISTANCE / max_exact))
                             * np.float32(NUM_BUCKETS - max_exact)).astype(np.int32)
        large = np.minimum(large, NUM_BUCKETS - 1)
        return np.where(n < max_exact, n, large).astype(np.int32)

    diag = np.where(c - r >= 0, bucket(c - r), -1)
    left = bucket(t + c - r)
    return np.stack([diag, left]).astype(np.int32)


def _bias_kernel(rb_ref, bucket_ref, out_ref):
    h = pl.program_id(0)
    bucket = bucket_ref[0]
    far = rb_ref[NUM_BUCKETS - 1, h]
    acc = jnp.full(bucket.shape, MASKED, _F32)
    for b in range(NUM_BUCKETS):
        acc = jnp.where(bucket == b, rb_ref[b, h] - far, acc)
    out_ref[0, 0] = acc


def _bias_tiles(rel_bias):
    nh = rel_bias.shape[1]
    t = ATTN_TILE
    buckets = jnp.asarray(_bucket_tiles())
    return pl.pallas_call(
        _bias_kernel,
        grid=(nh, 2),
        in_specs=[pl.BlockSpec(memory_space=pltpu.SMEM),
                  pl.BlockSpec((1, t, t), lambda h, j: (j, 0, 0))],
        out_specs=pl.BlockSpec((1, 1, t, t), lambda h, j: (h, j, 0, 0)),
        out_shape=jax.ShapeDtypeStruct((nh, 2, t, t), _F32),
        compiler_params=_params("parallel", "parallel"),
        name="bias_tiles",
    )(rel_bias, buckets)


def _attn_kernel(q_ref, k_ref, vt_ref, bias_ref, lq1_ref, lk1_ref, lq2_ref, lk2_ref, g_ref,
                 o_ref, m_ref, l_ref, acc_ref, *, lambda_init):
    t = ATTN_TILE
    i = pl.program_id(2)
    q = q_ref[...]
    lane = lax.broadcasted_iota(jnp.int32, q.shape, 1)
    zero = jnp.zeros_like(q)
    qz = (jnp.where(lane < HEAD_DIM, q, zero), jnp.where(lane >= HEAD_DIM, q, zero))

    m_ref[...] = jnp.full(m_ref.shape, -jnp.inf, _F32)
    l_ref[...] = jnp.zeros(l_ref.shape, _F32)
    acc_ref[...] = jnp.zeros(acc_ref.shape, _F32)

    def step(j, bias_idx):
        kk = k_ref[pl.ds(pl.multiple_of(j * t, t), t), :]
        vt = vt_ref[0, j]
        for a in range(2):
            s = _dot_nt(kk, qz[a])
            if bias_idx is not None:
                s = s + bias_ref[0, bias_idx]
            m_old = m_ref[a]
            m_new = jnp.maximum(m_old, jnp.max(s, axis=0, keepdims=True))
            alpha = jnp.exp(m_old - m_new)
            p = jnp.exp(s - m_new)
            l_ref[a] = alpha * l_ref[a] + jnp.sum(p, axis=0, keepdims=True)
            acc_ref[a] = alpha * acc_ref[a] + _dot(vt, p.astype(_BF16))
            m_ref[a] = m_new

    def far_body(j, carry):
        step(j, None)
        return carry

    lax.fori_loop(0, i - 1, far_body, 0)

    @pl.when(i >= 1)
    def _():
        step(i - 1, 1)

    step(i, 0)

    lam = (jnp.exp(jnp.sum(lq1_ref[...] * lk1_ref[...])) - jnp.exp(jnp.sum(lq2_ref[...] * lk2_ref[...]))
           + lambda_init)
    o = acc_ref[0] / l_ref[0] - lam * (acc_ref[1] / l_ref[1])
    o = o * lax.rsqrt(jnp.mean(o * o, axis=0, keepdims=True) + EPS)
    o = o * g_ref[...] * (1.0 - lambda_init)
    o_ref[...] = o.T


def _attention(q, k, vt, bias, lq1, lk1, lq2, lk2, subln_g, *, batch, seq, lambda_init):
    n, d_qk = q.shape
    t = ATTN_TILE
    nq = seq // t
    nh = d_qk // V_HEAD_DIM
    aw = vt.shape[2]
    vec = _resident((1, HEAD_DIM))
    return pl.pallas_call(
        functools.partial(_attn_kernel, lambda_init=lambda_init),
        grid=(batch, nh, nq),
        in_specs=[pl.BlockSpec((t, V_HEAD_DIM), lambda b, h, i: (b * nq + i, h)),
                  pl.BlockSpec((seq, V_HEAD_DIM), lambda b, h, i: (b, h)),
                  pl.BlockSpec((1, nq, V_HEAD_DIM, t), lambda b, h, i: (b, 0, h, 0)),
                  pl.BlockSpec((1, 2, t, t), lambda b, h, i: (h, 0, 0, 0)),
                  vec, vec, vec, vec, _resident((V_HEAD_DIM, 1))],
        out_specs=pl.BlockSpec((t, V_HEAD_DIM), lambda b, h, i: (b * nq + i, h)),
        out_shape=jax.ShapeDtypeStruct((n, aw), _F32),
        scratch_shapes=[pltpu.VMEM((2, 1, t), _F32), pltpu.VMEM((2, 1, t), _F32),
                        pltpu.VMEM((2, V_HEAD_DIM, t), _F32)],
        compiler_params=_params("parallel", "parallel", "arbitrary"),
        name="diff_attn",
    )(q, k, vt, bias, lq1, lk1, lq2, lk2, subln_g)


def _conv_kernel(c_ref, halo_ref, w_ref, cb_ref, lng_ref, lnb_ref, wpw_ref, bpw_ref, o_ref,
                 pad_ref, sh_ref, y_ref):
    rows = c_ref.shape[0]
    first = pl.program_id(1) == 0
    halo = halo_ref[...]
    pad_ref[:CONV_HALO] = jnp.where(first, jnp.zeros_like(halo), halo)
    pad_ref[CONV_HALO:] = c_ref[...]
    span = rows + CONV_HALO - 8
    for b in range(1, 8):
        sh_ref[b - 1, :span] = pad_ref[b:b + span]

    lead = CONV_HALO - (CONV_WIDTH - 1)

    def chunk(ci, carry):
        r0 = pl.multiple_of(ci * CONV_CHUNK, CONV_CHUNK)
        acc = jnp.zeros((CONV_CHUNK, c_ref.shape[1]), _F32)
        for kt in range(CONV_WIDTH):
            a, b = divmod(kt + lead, 8)
            if b == 0:
                xs = pad_ref[pl.ds(r0 + 8 * a, CONV_CHUNK), :]
            else:
                xs = sh_ref[b - 1, pl.ds(r0 + 8 * a, CONV_CHUNK), :]
            acc = acc + xs * w_ref[kt:kt + 1, :]
        y_ref[pl.ds(r0, CONV_CHUNK), :] = acc + cb_ref[...]
        return carry

    lax.fori_loop(0, rows // CONV_CHUNK, chunk, 0)

    y = y_ref[...]
    mu = jnp.mean(y, axis=-1, keepdims=True)
    yc = y - mu
    var = jnp.mean(yc * yc, axis=-1, keepdims=True)
    z = yc * lax.rsqrt(var + EPS) * lng_ref[...] + lnb_ref[...]
    z = z * jax.nn.sigmoid(z)
    o_ref[...] = _dot(z.astype(_BF16), wpw_ref[...]) + bpw_ref[...]


def _conv_branch(c, conv_w, conv_b, ln_g, ln_b, wpw, bpw, *, batch, seq):
    n, ch = c.shape
    d = wpw.shape[1]
    rows = CONV_ROWS
    nt = seq // rows
    halo_per_tile = rows // CONV_HALO
    return pl.pallas_call(
        _conv_kernel,
        grid=(batch, nt),
        in_specs=[pl.BlockSpec((rows, ch), lambda b, i: (b * nt + i, 0)),
                  pl.BlockSpec((CONV_HALO, ch),
                               lambda b, i: (jnp.maximum((b * nt + i) * halo_per_tile - 1, 0), 0)),
                  _resident(conv_w.shape), _resident((1, ch)), _resident((1, ch)), _resident((1, ch)),
                  _resident(wpw.shape), _resident((1, d))],
        out_specs=pl.BlockSpec((rows, d), lambda b, i: (b * nt + i, 0)),
        out_shape=jax.ShapeDtypeStruct((n, d), _F32),
        scratch_shapes=[pltpu.VMEM((rows + CONV_HALO, ch), _F32),
                        pltpu.VMEM((7, rows + CONV_HALO, ch), _F32),
                        pltpu.VMEM((rows, ch), _F32)],
        compiler_params=_params("parallel", "arbitrary"),
        name="conv_branch",
    )(c, c, conv_w, conv_b, ln_g, ln_b, wpw, bpw)


def _mlp_kernel(x_ref, attn_ref, conv_ref, gate_ref, wout_ref, g2_ref, wup_ref, wdown_ref, gf_ref,
                o_ref, *, final_norm):
    d = x_ref.shape[1]
    gates = gate_ref[...].astype(_F32)
    merged = gates[:, :d] * attn_ref[...] + gates[:, d:] * conv_ref[...]
    h1 = x_ref[...] + _dot(merged.astype(_BF16), wout_ref[...])
    u = (_rms(h1) * g2_ref[...]).astype(_BF16)
    dff = wup_ref.shape[1]
    acc = jnp.zeros(h1.shape, _F32)
    for c0 in range(0, dff, d):
        hid = jnp.maximum(_dot(u, wup_ref[:, c0:c0 + d]), 0.0)
        acc = acc + _dot((hid * hid).astype(_BF16), wdown_ref[c0:c0 + d, :])
    h2 = h1 + acc
    if final_norm:
        h2 = _rms(h2) * gf_ref[...]
    o_ref[...] = h2


def _merge_mlp(x2, attn, conv_out, gates, wout, g2, wup, wdown, gf, *, final_norm):
    n, d = x2.shape
    rows = MLP_ROWS
    row_spec = lambda w: pl.BlockSpec((rows, w), lambda i: (i, 0))
    return pl.pallas_call(
        functools.partial(_mlp_kernel, final_norm=final_norm),
        grid=(n // rows,),
        in_specs=[row_spec(d), row_spec(d), row_spec(d), row_spec(2 * d), _resident(wout.shape),
                  _resident((1, d)), _resident(wup.shape), _resident(wdown.shape), _resident((1, d))],
        out_specs=row_spec(d),
        out_shape=jax.ShapeDtypeStruct((n, d), _F32),
        compiler_params=_params("parallel"),
        name="merge_mlp",
    )(x2, attn, conv_out, gates, wout, g2, wup, wdown, gf)


def kernel(x, rel_bias, final_norm_g, norm_mix_g, w_in, b_glu, b_gate, lam_q1, lam_k1, lam_q2, lam_k2,
           subln_g, conv_w, conv_b, conv_ln_g, conv_ln_b, w_pw2, b_pw2, w_out, norm_mlp_g, w_up, w_down):
    batch, seq, d = x.shape
    depth = w_in.shape[0]
    ch = conv_w.shape[2]
    d_qk = (w_in.shape[2] - 2 * ch - 2 * d) // 3
    c1, c2, c3 = 2 * d_qk, 3 * d_qk, 3 * d_qk + 2 * ch
    assert seq % ATTN_TILE == 0 and seq % PROJ_ROWS == 0 and PROJ_ROWS % ATTN_TILE == 0
    assert seq % CONV_ROWS == 0 and (batch * seq) % MLP_ROWS == 0 and ATTN_TILE >= MAX_DISTANCE

    row = lambda v: v.reshape(1, -1)
    h = x.reshape(batch * seq, d)
    bias = _bias_tiles(rel_bias)
    for l in range(depth):
        lambda_init = 0.8 - 0.6 * math.exp(-0.3 * l)
        w = w_in[l]
        q, k, vt, c, gates = _proj(
            h, row(norm_mix_g[l]), w[:, :c1].astype(_BF16), w[:, c1:c2].T.astype(_BF16),
            w[:, c2:c3].astype(_BF16), w[:, c3:].astype(_BF16), row(b_glu[l]), row(b_gate[l]),
            batch=batch, seq=seq)
        attn = _attention(q, k, vt, bias, row(lam_q1[l]), row(lam_k1[l]), row(lam_q2[l]), row(lam_k2[l]),
                          subln_g[l].reshape(-1, 1), batch=batch, seq=seq, lambda_init=lambda_init)
        conv_out = _conv_branch(c, conv_w[l], row(conv_b[l]), row(conv_ln_g[l]), row(conv_ln_b[l]),
                                w_pw2[l].astype(_BF16), row(b_pw2[l]), batch=batch, seq=seq)
        h = _merge_mlp(h, attn, conv_out, gates, w_out[l].astype(_BF16), row(norm_mlp_g[l]),
                       w_up[l].astype(_BF16), w_down[l].astype(_BF16), row(final_norm_g),
                       final_norm=(l == depth - 1))
    return h.reshape(batch, seq, d)
```

```python
import functools
import math

import jax
import jax.numpy as jnp
import numpy as np
from jax import lax
from jax.experimental import pallas as pl
from jax.experimental.pallas import tpu as pltpu

HEAD_DIM = 64
V_HEAD_DIM = 2 * HEAD_DIM
CONV_WIDTH = 31
NUM_BUCKETS = 32
MAX_DISTANCE = 128
EPS = 1e-6
MASKED = -1e30
LOG2E = 1.4426950408889634

PROJ_ROWS = 512
ATTN_TILE = 512
CONV_ROWS = 256
CONV_HALO = 32
CONV_CHUNK = 32
MLP_ROWS = 512
VMEM_LIMIT = 56 * 1024 * 1024

_BF16 = jnp.bfloat16
_F32 = jnp.float32


def _resident(shape):
    return pl.BlockSpec(shape, lambda *_: (0,) * len(shape), pipeline_mode=pl.Buffered(1))


def _params(*semantics):
    return pltpu.CompilerParams(dimension_semantics=semantics, vmem_limit_bytes=VMEM_LIMIT)


def _rms(xf):
    return xf * lax.rsqrt(jnp.mean(xf * xf, axis=-1, keepdims=True) + EPS)


def _dot(a, b):
    return jnp.dot(a, b, preferred_element_type=_F32)


def _dot_nt(a, b):
    return lax.dot_general(a, b, (((1,), (1,)), ((), ())), preferred_element_type=_F32)


def _proj_kernel(x_ref, g_ref, wqk_ref, wvt_ref, wglu_ref, wgate_ref, bglu_ref, bgate_ref,
                 q_ref, k_ref, vt_ref, c_ref, gate_ref, *, d_qk, kv_tiles_per_block):
    u = (_rms(x_ref[...]) * g_ref[...]).astype(_BF16)
    scale = LOG2E / math.sqrt(HEAD_DIM)
    q_ref[...] = (_dot(u, wqk_ref[:, :d_qk]) * scale).astype(_BF16)
    k_ref[...] = _dot(u, wqk_ref[:, d_qk:]).astype(_BF16)
    vt = _dot_nt(wvt_ref[...], u).astype(_BF16)
    for t in range(kv_tiles_per_block):
        vt_ref[0, t] = vt[:, t * ATTN_TILE:(t + 1) * ATTN_TILE]
    ch = c_ref.shape[-1]
    ga = _dot(u, wglu_ref[:, :ch]) + bglu_ref[:, :ch]
    gb = _dot(u, wglu_ref[:, ch:]) + bglu_ref[:, ch:]
    c_ref[...] = ga * jax.nn.sigmoid(gb)
    gate_ref[...] = jax.nn.sigmoid(_dot(u, wgate_ref[...]) + bgate_ref[...]).astype(_BF16)


def _proj(x2, g, wqk, wvt, wglu, wgate, bglu, bgate, *, batch, seq):
    n, d = x2.shape
    d_qk = wqk.shape[1] // 2
    aw = wvt.shape[0]
    ch = wglu.shape[1] // 2
    ngate = wgate.shape[1]
    rows = PROJ_ROWS
    tiles = rows // ATTN_TILE
    blocks_per_seq = seq // rows
    row_spec = lambda w: pl.BlockSpec((rows, w), lambda i: (i, 0))
    return pl.pallas_call(
        functools.partial(_proj_kernel, d_qk=d_qk, kv_tiles_per_block=tiles),
        grid=(n // rows,),
        in_specs=[row_spec(d), _resident((1, d)), _resident(wqk.shape), _resident(wvt.shape),
                  _resident(wglu.shape), _resident(wgate.shape), _resident(bglu.shape),
                  _resident(bgate.shape)],
        out_specs=[row_spec(d_qk), row_spec(d_qk),
                   pl.BlockSpec((1, tiles, aw, ATTN_TILE),
                                lambda i: (i // blocks_per_seq, i % blocks_per_seq, 0, 0)),
                   row_spec(ch), row_spec(ngate)],
        out_shape=[jax.ShapeDtypeStruct((n, d_qk), _BF16), jax.ShapeDtypeStruct((n, d_qk), _BF16),
                   jax.ShapeDtypeStruct((batch, seq // ATTN_TILE, aw, ATTN_TILE), _BF16),
                   jax.ShapeDtypeStruct((n, ch), _F32), jax.ShapeDtypeStruct((n, ngate), _BF16)],
        compiler_params=_params("parallel"),
        name="proj",
    )(x2, g, wqk, wvt, wglu, wgate, bglu, bgate)


def _bucket_tiles():
    t = ATTN_TILE
    r = np.arange(t, dtype=np.int64)[:, None]
    c = np.arange(t, dtype=np.int64)[None, :]

    def bucket(rel):
        n = np.maximum(rel, 0)
        max_exact = NUM_BUCKETS // 2
        nf = np.maximum(n, 1).astype(np.float32)
        large = max_exact + (np.log(nf / np.float32(max_exact)) / np.float32(math.log(MAX_DISTANCE / max_exact))
                             * np.float32(NUM_BUCKETS - max_exact)).astype(np.int32)
        large = np.minimum(large, NUM_BUCKETS - 1)
        return np.where(n < max_exact, n, large).astype(np.int32)

    diag = np.where(c - r >= 0, bucket(c - r), -1)
    left = bucket(t + c - r)
    return np.stack([diag, left]).astype(np.int32)


def _bias_kernel(rb_ref, bucket_ref, out_ref):
    h = pl.program_id(0)
    bucket = bucket_ref[0]
    far = rb_ref[NUM_BUCKETS - 1, h]
    acc = jnp.full(bucket.shape, MASKED, _F32)
    for b in range(NUM_BUCKETS):
        acc = jnp.where(bucket == b, (rb_ref[b, h] - far) * LOG2E, acc)
    out_ref[0, 0] = acc


def _bias_tiles(rel_bias):
    nh = rel_bias.shape[1]
    t = ATTN_TILE
    buckets = jnp.asarray(_bucket_tiles())
    return pl.pallas_call(
        _bias_kernel,
        grid=(nh, 2),
        in_specs=[pl.BlockSpec(memory_space=pltpu.SMEM),
                  pl.BlockSpec((1, t, t), lambda h, j: (j, 0, 0))],
        out_specs=pl.BlockSpec((1, 1, t, t), lambda h, j: (h, j, 0, 0)),
        out_shape=jax.ShapeDtypeStruct((nh, 2, t, t), _F32),
        compiler_params=_params("parallel", "parallel"),
        name="bias_tiles",
    )(rel_bias, buckets)


_FAR, _LEFT, _DIAG, _LAST = 0, 1, 2, 3
_SUM_ROWS = 16


def _attn_steps(nq):
    steps = []
    for i in range(nq):
        for j in range(i + 1):
            kind = _DIAG if j == i else (_LEFT if j == i - 1 else _FAR)
            steps.append((i, j, kind))
    steps[-1] = (nq - 1, nq - 1, _LAST)
    steps.append((0, 0, _FAR))
    return np.asarray(steps, dtype=np.int32).T.copy()


def _attn_kernel(tab_ref, q_ref, k_ref, vt_ref, bias_ref, lq1_ref, lk1_ref, lq2_ref, lk2_ref, g_ref,
                 o_ref, s0_ref, s1_ref, m_ref, acc_ref, *, lambda_init, nsteps):
    t = ATTN_TILE
    s_ref = (s0_ref, s1_ref)

    def scores(step, slot):
        i = tab_ref[0, step]
        j = tab_ref[1, step]
        q = q_ref[pl.ds(pl.multiple_of(i * t, t), t), :]
        kk = k_ref[pl.ds(pl.multiple_of(j * t, t), t), :]
        lane = lax.broadcasted_iota(jnp.int32, q.shape, 1)
        zero = jnp.zeros_like(q)
        s_ref[slot][0] = _dot_nt(kk, jnp.where(lane < HEAD_DIM, q, zero))
        s_ref[slot][1] = _dot_nt(kk, jnp.where(lane >= HEAD_DIM, q, zero))

    def finalize(i):
        lam = (jnp.exp(jnp.sum(lq1_ref[...] * lk1_ref[...])) - jnp.exp(jnp.sum(lq2_ref[...] * lk2_ref[...]))
               + lambda_init)
        a1 = acc_ref[0]
        a2 = acc_ref[1]
        r1 = 1.0 / a1[V_HEAD_DIM:V_HEAD_DIM + 1]
        r2 = lam / a2[V_HEAD_DIM:V_HEAD_DIM + 1]
        o = a1[:V_HEAD_DIM] * r1 - a2[:V_HEAD_DIM] * r2
        o = o * lax.rsqrt(jnp.mean(o * o, axis=0, keepdims=True) + EPS)
        o = o * (g_ref[...] * (1.0 - lambda_init))
        o_ref[pl.ds(pl.multiple_of(i * t, t), t), :] = o.T

    def process(step, slot, kind):
        i = tab_ref[0, step]
        j = tab_ref[1, step]
        vt = jnp.concatenate([vt_ref[0, j], jnp.ones((_SUM_ROWS, t), _BF16)], axis=0)
        for a in range(2):
            s = s_ref[slot][a]
            if kind == _LEFT:
                s = s + bias_ref[0, 1]
            elif kind in (_DIAG, _LAST):
                s = s + bias_ref[0, 0]
            m_old = jnp.where(j == 0, -jnp.inf, m_ref[a])
            m_new = jnp.maximum(m_old, jnp.max(s, axis=0, keepdims=True))
            alpha = jnp.exp2(m_old - m_new)
            p = jnp.exp2(s - m_new).astype(_BF16)
            acc_ref[a] = alpha * acc_ref[a] + _dot(vt, p)
            m_ref[a] = m_new
        if kind in (_DIAG, _LAST):
            finalize(i)

    acc_ref[...] = jnp.zeros(acc_ref.shape, _F32)
    m_ref[...] = jnp.full(m_ref.shape, -jnp.inf, _F32)
    scores(0, 0)

    def one_step(step, slot):
        kind = tab_ref[2, step]
        for kd in (_FAR, _LEFT, _DIAG, _LAST):
            @pl.when(kind == kd)
            def _():
                if kd != _LAST:
                    scores(step + 1, 1 - slot)
                process(step, slot, kd)

    def body(pair, carry):
        one_step(2 * pair, 0)
        one_step(2 * pair + 1, 1)
        return carry

    lax.fori_loop(0, nsteps // 2, body, 0)
    if nsteps % 2:
        one_step(nsteps - 1, 0)


def _attention(q, k, vt, bias, lq1, lk1, lq2, lk2, subln_g, *, batch, seq, lambda_init):
    n, d_qk = q.shape
    t = ATTN_TILE
    nq = seq // t
    nh = d_qk // V_HEAD_DIM
    aw = vt.shape[2]
    tab = jnp.asarray(_attn_steps(nq))
    vec = _resident((1, HEAD_DIM))
    head_rows = pl.BlockSpec((seq, V_HEAD_DIM), lambda b, h: (b, h))
    return pl.pallas_call(
        functools.partial(_attn_kernel, lambda_init=lambda_init, nsteps=tab.shape[1] - 1),
        grid=(batch, nh),
        in_specs=[pl.BlockSpec(memory_space=pltpu.SMEM), head_rows, head_rows,
                  pl.BlockSpec((1, nq, V_HEAD_DIM, t), lambda b, h: (b, 0, h, 0)),
                  pl.BlockSpec((1, 2, t, t), lambda b, h: (h, 0, 0, 0)),
                  vec, vec, vec, vec, _resident((V_HEAD_DIM, 1))],
        out_specs=head_rows,
        out_shape=jax.ShapeDtypeStruct((n, aw), _F32),
        scratch_shapes=[pltpu.VMEM((2, t, t), _F32), pltpu.VMEM((2, t, t), _F32), pltpu.VMEM((2, 1, t), _F32),
                        pltpu.VMEM((2, V_HEAD_DIM + _SUM_ROWS, t), _F32)],
        compiler_params=_params("parallel", "parallel"),
        name="diff_attn",
    )(tab, q, k, vt, bias, lq1, lk1, lq2, lk2, subln_g)


def _conv_kernel(c_ref, halo_ref, w_ref, cb_ref, lng_ref, lnb_ref, wpw_ref, bpw_ref, o_ref,
                 pad_ref, sh_ref, y_ref):
    rows = c_ref.shape[0]
    first = pl.program_id(1) == 0
    halo = halo_ref[...]
    pad_ref[:CONV_HALO] = jnp.where(first, jnp.zeros_like(halo), halo)
    pad_ref[CONV_HALO:] = c_ref[...]
    span = rows + CONV_HALO - 8
    for b in range(1, 8):
        sh_ref[b - 1, :span] = pad_ref[b:b + span]

    lead = CONV_HALO - (CONV_WIDTH - 1)

    def chunk(ci, carry):
        r0 = pl.multiple_of(ci * CONV_CHUNK, CONV_CHUNK)
        acc = jnp.zeros((CONV_CHUNK, c_ref.shape[1]), _F32)
        for kt in range(CONV_WIDTH):
            a, b = divmod(kt + lead, 8)
            if b == 0:
                xs = pad_ref[pl.ds(r0 + 8 * a, CONV_CHUNK), :]
            else:
                xs = sh_ref[b - 1, pl.ds(r0 + 8 * a, CONV_CHUNK), :]
            acc = acc + xs * w_ref[kt:kt + 1, :]
        y_ref[pl.ds(r0, CONV_CHUNK), :] = acc + cb_ref[...]
        return carry

    lax.fori_loop(0, rows // CONV_CHUNK, chunk, 0)

    y = y_ref[...]
    mu = jnp.mean(y, axis=-1, keepdims=True)
    yc = y - mu
    var = jnp.mean(yc * yc, axis=-1, keepdims=True)
    z = yc * lax.rsqrt(var + EPS) * lng_ref[...] + lnb_ref[...]
    z = z * jax.nn.sigmoid(z)
    o_ref[...] = _dot(z.astype(_BF16), wpw_ref[...]) + bpw_ref[...]


def _conv_branch(c, conv_w, conv_b, ln_g, ln_b, wpw, bpw, *, batch, seq):
    n, ch = c.shape
    d = wpw.shape[1]
    rows = CONV_ROWS
    nt = seq // rows
    halo_per_tile = rows // CONV_HALO
    return pl.pallas_call(
        _conv_kernel,
        grid=(batch, nt),
        in_specs=[pl.BlockSpec((rows, ch), lambda b, i: (b * nt + i, 0)),
                  pl.BlockSpec((CONV_HALO, ch),
                               lambda b, i: (jnp.maximum((b * nt + i) * halo_per_tile - 1, 0), 0)),
                  _resident(conv_w.shape), _resident((1, ch)), _resident((1, ch)), _resident((1, ch)),
                  _resident(wpw.shape), _resident((1, d))],
        out_specs=pl.BlockSpec((rows, d), lambda b, i: (b * nt + i, 0)),
        out_shape=jax.ShapeDtypeStruct((n, d), _F32),
        scratch_shapes=[pltpu.VMEM((rows + CONV_HALO, ch), _F32),
                        pltpu.VMEM((7, rows + CONV_HALO, ch), _F32),
                        pltpu.VMEM((rows, ch), _F32)],
        compiler_params=_params("parallel", "arbitrary"),
        name="conv_branch",
    )(c, c, conv_w, conv_b, ln_g, ln_b, wpw, bpw)


def _mlp_kernel(x_ref, attn_ref, conv_ref, gate_ref, wout_ref, g2_ref, wup_ref, wdown_ref, gf_ref,
                o_ref, *, final_norm):
    d = x_ref.shape[1]
    gates = gate_ref[...].astype(_F32)
    merged = gates[:, :d] * attn_ref[...] + gates[:, d:] * conv_ref[...]
    h1 = x_ref[...] + _dot(merged.astype(_BF16), wout_ref[...])
    u = (_rms(h1) * g2_ref[...]).astype(_BF16)
    dff = wup_ref.shape[1]
    acc = jnp.zeros(h1.shape, _F32)
    for c0 in range(0, dff, d):
        hid = jnp.maximum(_dot(u, wup_ref[:, c0:c0 + d]), 0.0)
        acc = acc + _dot((hid * hid).astype(_BF16), wdown_ref[c0:c0 + d, :])
    h2 = h1 + acc
    if final_norm:
        h2 = _rms(h2) * gf_ref[...]
    o_ref[...] = h2


def _merge_mlp(x2, attn, conv_out, gates, wout, g2, wup, wdown, gf, *, final_norm):
    n, d = x2.shape
    rows = MLP_ROWS
    row_spec = lambda w: pl.BlockSpec((rows, w), lambda i: (i, 0))
    return pl.pallas_call(
        functools.partial(_mlp_kernel, final_norm=final_norm),
        grid=(n // rows,),
        in_specs=[row_spec(d), row_spec(d), row_spec(d), row_spec(2 * d), _resident(wout.shape),
                  _resident((1, d)), _resident(wup.shape), _resident(wdown.shape), _resident((1, d))],
        out_specs=row_spec(d),
        out_shape=jax.ShapeDtypeStruct((n, d), _F32),
        compiler_params=_params("parallel"),
        name="merge_mlp",
    )(x2, attn, conv_out, gates, wout, g2, wup, wdown, gf)


def kernel(x, rel_bias, final_norm_g, norm_mix_g, w_in, b_glu, b_gate, lam_q1, lam_k1, lam_q2, lam_k2,
           subln_g, conv_w, conv_b, conv_ln_g, conv_ln_b, w_pw2, b_pw2, w_out, norm_mlp_g, w_up, w_down):
    batch, seq, d = x.shape
    depth = w_in.shape[0]
    ch = conv_w.shape[2]
    d_qk = (w_in.shape[2] - 2 * ch - 2 * d) // 3
    c1, c2, c3 = 2 * d_qk, 3 * d_qk, 3 * d_qk + 2 * ch
    assert seq % ATTN_TILE == 0 and seq % PROJ_ROWS == 0 and PROJ_ROWS % ATTN_TILE == 0
    assert seq % CONV_ROWS == 0 and (batch * seq) % MLP_ROWS == 0 and ATTN_TILE >= MAX_DISTANCE

    row = lambda v: v.reshape(1, -1)
    h = x.reshape(batch * seq, d)
    bias = _bias_tiles(rel_bias)
    for l in range(depth):
        lambda_init = 0.8 - 0.6 * math.exp(-0.3 * l)
        w = w_in[l]
        q, k, vt, c, gates = _proj(
            h, row(norm_mix_g[l]), w[:, :c1].astype(_BF16), w[:, c1:c2].T.astype(_BF16),
            w[:, c2:c3].astype(_BF16), w[:, c3:].astype(_BF16), row(b_glu[l]), row(b_gate[l]),
            batch=batch, seq=seq)
        attn = _attention(q, k, vt, bias, row(lam_q1[l]), row(lam_k1[l]), row(lam_q2[l]), row(lam_k2[l]),
                          subln_g[l].reshape(-1, 1), batch=batch, seq=seq, lambda_init=lambda_init)
        conv_out = _conv_branch(c, conv_w[l], row(conv_b[l]), row(conv_ln_g[l]), row(conv_ln_b[l]),
                                w_pw2[l].astype(_BF16), row(b_pw2[l]), batch=batch, seq=seq)
        h = _merge_mlp(h, attn, conv_out, gates, w_out[l].astype(_BF16), row(norm_mlp_g[l]),
                       w_up[l].astype(_BF16), w_down[l].astype(_BF16), row(final_norm_g),
                       final_norm=(l == depth - 1))
    return h.reshape(batch, seq, d)
```

```python
import functools
import math

import jax
import jax.numpy as jnp
import numpy as np
from jax import lax
from jax.experimental import pallas as pl
from jax.experimental.pallas import tpu as pltpu

HEAD_DIM = 64
V_HEAD_DIM = 2 * HEAD_DIM
CONV_WIDTH = 31
NUM_BUCKETS = 32
MAX_DISTANCE = 128
EPS = 1e-6
MASKED = -1e30
LOG2E = 1.4426950408889634

PROJ_ROWS = 512
ATTN_TILE = 512
ATTN_COLS = 256
ATTN_HEADS = 2
CONV_ROWS = 256
CONV_HALO = 32
CONV_CHUNK = 32
MLP_ROWS = 512
VMEM_LIMIT = 56 * 1024 * 1024

_BF16 = jnp.bfloat16
_F32 = jnp.float32


def _resident(shape):
    return pl.BlockSpec(shape, lambda *_: (0,) * len(shape), pipeline_mode=pl.Buffered(1))


def _params(*semantics):
    return pltpu.CompilerParams(dimension_semantics=semantics, vmem_limit_bytes=VMEM_LIMIT)


def _rms(xf):
    return xf * lax.rsqrt(jnp.mean(xf * xf, axis=-1, keepdims=True) + EPS)


def _dot(a, b):
    return jnp.dot(a, b, preferred_element_type=_F32)


def _dot_nt(a, b):
    return lax.dot_general(a, b, (((1,), (1,)), ((), ())), preferred_element_type=_F32)


def _proj_kernel(x_ref, g_ref, wqk_ref, wvt_ref, wglu_ref, wgate_ref, bglu_ref, bgate_ref,
                 q_ref, k_ref, vt_ref, c_ref, gate_ref, *, d_qk, kv_tiles_per_block):
    u = (_rms(x_ref[...]) * g_ref[...]).astype(_BF16)
    scale = LOG2E / math.sqrt(HEAD_DIM)
    q_ref[...] = (_dot(u, wqk_ref[:, :d_qk]) * scale).astype(_BF16)
    k_ref[...] = _dot(u, wqk_ref[:, d_qk:]).astype(_BF16)
    vt = _dot_nt(wvt_ref[...], u).astype(_BF16)
    for t in range(kv_tiles_per_block):
        vt_ref[0, t] = vt[:, t * ATTN_TILE:(t + 1) * ATTN_TILE]
    ch = c_ref.shape[-1]
    ga = _dot(u, wglu_ref[:, :ch]) + bglu_ref[:, :ch]
    gb = _dot(u, wglu_ref[:, ch:]) + bglu_ref[:, ch:]
    c_ref[...] = ga * jax.nn.sigmoid(gb)
    gate_ref[...] = jax.nn.sigmoid(_dot(u, wgate_ref[...]) + bgate_ref[...]).astype(_BF16)


def _proj(x2, g, wqk, wvt, wglu, wgate, bglu, bgate, *, batch, seq):
    n, d = x2.shape
    d_qk = wqk.shape[1] // 2
    aw = wvt.shape[0]
    ch = wglu.shape[1] // 2
    ngate = wgate.shape[1]
    rows = PROJ_ROWS
    tiles = rows // ATTN_TILE
    blocks_per_seq = seq // rows
    row_spec = lambda w: pl.BlockSpec((rows, w), lambda i: (i, 0))
    return pl.pallas_call(
        functools.partial(_proj_kernel, d_qk=d_qk, kv_tiles_per_block=tiles),
        grid=(n // rows,),
        in_specs=[row_spec(d), _resident((1, d)), _resident(wqk.shape), _resident(wvt.shape),
                  _resident(wglu.shape), _resident(wgate.shape), _resident(bglu.shape),
                  _resident(bgate.shape)],
        out_specs=[row_spec(d_qk), row_spec(d_qk),
                   pl.BlockSpec((1, tiles, aw, ATTN_TILE),
                                lambda i: (i // blocks_per_seq, i % blocks_per_seq, 0, 0)),
                   row_spec(ch), row_spec(ngate)],
        out_shape=[jax.ShapeDtypeStruct((n, d_qk), _BF16), jax.ShapeDtypeStruct((n, d_qk), _BF16),
                   jax.ShapeDtypeStruct((batch, seq // ATTN_TILE, aw, ATTN_TILE), _BF16),
                   jax.ShapeDtypeStruct((n, ch), _F32), jax.ShapeDtypeStruct((n, ngate), _BF16)],
        compiler_params=_params("parallel"),
        name="proj",
    )(x2, g, wqk, wvt, wglu, wgate, bglu, bgate)


def _bucket_tiles():
    t = ATTN_TILE
    r = np.arange(t, dtype=np.int64)[:, None]
    c = np.arange(t, dtype=np.int64)[None, :]

    def bucket(rel):
        n = np.maximum(rel, 0)
        max_exact = NUM_BUCKETS // 2
        nf = np.maximum(n, 1).astype(np.float32)
        large = max_exact + (np.log(nf / np.float32(max_exact)) / np.float32(math.log(MAX_DISTANCE / max_exact))
                             * np.float32(NUM_BUCKETS - max_exact)).astype(np.int32)
        large = np.minimum(large, NUM_BUCKETS - 1)
        return np.where(n < max_exact, n, large).astype(np.int32)

    diag = np.where(c - r >= 0, bucket(c - r), -1)
    left = bucket(t + c - r)
    return np.stack([diag, left]).astype(np.int32)


def _bias_kernel(rb_ref, bucket_ref, out_ref):
    h = pl.program_id(0)
    bucket = bucket_ref[0]
    far = rb_ref[NUM_BUCKETS - 1, h]
    acc = jnp.full(bucket.shape, MASKED, _F32)
    for b in range(NUM_BUCKETS):
        acc = jnp.where(bucket == b, (rb_ref[b, h] - far) * LOG2E, acc)
    out_ref[0, 0] = acc


def _bias_tiles(rel_bias):
    nh = rel_bias.shape[1]
    t = ATTN_TILE
    buckets = jnp.asarray(_bucket_tiles())
    return pl.pallas_call(
        _bias_kernel,
        grid=(nh, 2),
        in_specs=[pl.BlockSpec(memory_space=pltpu.SMEM),
                  pl.BlockSpec((1, t, t), lambda h, j: (j, 0, 0))],
        out_specs=pl.BlockSpec((1, 1, t, t), lambda h, j: (h, j, 0, 0)),
        out_shape=jax.ShapeDtypeStruct((nh, 2, t, t), _F32),
        compiler_params=_params("parallel", "parallel"),
        name="bias_tiles",
    )(rel_bias, buckets)


SUBLANES = 8
MAX_CHAINS = 8
_PLAIN, _DIAG = 0, 1
_SUM_ROWS = 16


def _attn_steps(nq):
    steps = [[i, j, _DIAG if j == i else _PLAIN, 0] for i in range(nq) for j in range(i + 1)]
    for n in range(1, len(steps)):
        steps[n][3] = int(steps[n - 1][2] == _DIAG)
    return np.asarray(steps, dtype=np.int32).T.copy()


def _col_max(s):
    groups = [s[r:r + SUBLANES] for r in range(0, s.shape[0], SUBLANES)]
    acc = groups[:MAX_CHAINS]
    for n, g in enumerate(groups[MAX_CHAINS:]):
        acc[n % MAX_CHAINS] = jnp.maximum(acc[n % MAX_CHAINS], g)
    while len(acc) > 1:
        acc = [jnp.maximum(acc[n], acc[n + 1]) for n in range(0, len(acc), 2)]
    return jnp.max(acc[0], axis=0, keepdims=True)


def _tile_rows(idx, offset=0, size=ATTN_TILE):
    start = idx * ATTN_TILE + offset
    if not isinstance(start, int):
        start = pl.multiple_of(start, math.gcd(ATTN_TILE, offset))
    return pl.ds(start, size)


def _attn_kernel(tab_ref, q_ref, k_ref, vt_ref, bias_ref, lq1_ref, lk1_ref, lq2_ref, lk2_ref, g_ref,
                 o_ref, s0_ref, s1_ref, p0_ref, p1_ref, alpha_ref, m_ref, acc_ref, *, lambda_init, steps, heads):
    t, w = ATTN_TILE, ATTN_COLS
    s_ref = (s0_ref, s1_ref)
    p_ref = (p0_ref, p1_ref)
    nsteps = steps.shape[1]
    parts = [(hh, a, c) for hh in range(heads) for a in range(2) for c in range(0, t, w)]

    def head_cols(hh):
        return slice(hh * V_HEAD_DIM, (hh + 1) * V_HEAD_DIM)

    def scores(i, j, slot, hh, a, c):
        q = q_ref[_tile_rows(i, c, w), head_cols(hh)]
        kk = k_ref[_tile_rows(j), head_cols(hh)]
        lane = lax.broadcasted_iota(jnp.int32, q.shape, 1)
        keep = lane < HEAD_DIM if a == 0 else lane >= HEAD_DIM
        s_ref[slot][hh, a, :, c:c + w] = _dot_nt(kk, jnp.where(keep, q, jnp.zeros_like(q)))

    def softmax(i, j, slot, kind, hh, a, c):
        if kind == _PLAIN:
            if c == 0:
                near = jnp.where(j == i - 1, 1.0, 0.0).astype(_F32)
                s_ref[slot][hh, a, t - MAX_DISTANCE:, :MAX_DISTANCE] += (
                    bias_ref[hh, 1, t - MAX_DISTANCE:, :MAX_DISTANCE] * near)
            s = s_ref[slot][hh, a, :, c:c + w]
        else:
            s = s_ref[slot][hh, a, :, c:c + w] + bias_ref[hh, 0, :, c:c + w]
        m_old = jnp.where(j == 0, -jnp.inf, m_ref[hh, a, :, c:c + w])
        m_new = jnp.maximum(m_old, _col_max(s))
        alpha_ref[slot, hh, a, :, c:c + w] = jnp.exp2(m_old - m_new)
        p_ref[slot][hh, a, :, c:c + w] = jnp.exp2(s - m_new).astype(_BF16)
        m_ref[hh, a, :, c:c + w] = m_new

    def values(i, j, slot, hh, a, c):
        vt = jnp.concatenate([vt_ref[0, j, head_cols(hh), :], jnp.ones((_SUM_ROWS, t), _BF16)], axis=0)
        acc_ref[i % 2, hh, a, :, c:c + w] = (alpha_ref[slot, hh, a, :, c:c + w] * acc_ref[i % 2, hh, a, :, c:c + w]
                                             + _dot(vt, p_ref[slot][hh, a, :, c:c + w]))

    def finalize(i):
        lam = (jnp.exp(jnp.sum(lq1_ref[...] * lk1_ref[...])) - jnp.exp(jnp.sum(lq2_ref[...] * lk2_ref[...]))
               + lambda_init)
        for hh in range(heads):
            a1 = acc_ref[i % 2, hh, 0]
            a2 = acc_ref[i % 2, hh, 1]
            r1 = 1.0 / a1[V_HEAD_DIM:V_HEAD_DIM + 1]
            r2 = lam / a2[V_HEAD_DIM:V_HEAD_DIM + 1]
            o = a1[:V_HEAD_DIM] * r1 - a2[:V_HEAD_DIM] * r2
            o = o * lax.rsqrt(jnp.mean(o * o, axis=0, keepdims=True) + EPS)
            o = o * (g_ref[...] * (1.0 - lambda_init))
            o_ref[_tile_rows(i), head_cols(hh)] = o.T

    def pipeline_step(sc, sm, pv, fin):
        if fin is not None:
            finalize(fin)
        for hh, a, c in parts:
            if sc is not None:
                scores(sc[0], sc[1], sc[3], hh, a, c)
            if pv is not None:
                values(pv[0], pv[1], pv[3], hh, a, c)
            if sm is not None:
                softmax(sm[0], sm[1], sm[3], sm[2], hh, a, c)

    def static_step(n):
        return int(steps[0, n]), int(steps[1, n]), int(steps[2, n]), n % 2

    def static_fin(n):
        return int(steps[0, n - 1]) if n >= 1 and steps[3, n] else None

    acc_ref[...] = jnp.zeros(acc_ref.shape, _F32)
    m_ref[...] = jnp.full(m_ref.shape, -jnp.inf, _F32)

    pipeline_step(static_step(0), None, None, None)
    pipeline_step(static_step(1), static_step(0), None, None)

    full = nsteps - 2
    combos = sorted({(int(steps[2, n + 1]), int(steps[3, n])) for n in range(full)})

    def iteration(n, par):
        k_sm = tab_ref[2, n + 1]
        k_fin = tab_ref[3, n]
        for c_sm, c_fin in combos:
            @pl.when((k_sm == c_sm) & (k_fin == c_fin))
            def _():
                pipeline_step((tab_ref[0, n + 2], tab_ref[1, n + 2], None, par),
                              (tab_ref[0, n + 1], tab_ref[1, n + 1], c_sm, 1 - par),
                              (tab_ref[0, n], tab_ref[1, n], None, par),
                              tab_ref[0, jnp.maximum(n - 1, 0)] if c_fin else None)

    def body(pair, carry):
        iteration(2 * pair, 0)
        iteration(2 * pair + 1, 1)
        return carry

    lax.fori_loop(0, full // 2, body, 0)
    if full % 2:
        iteration(full - 1, (full - 1) % 2)

    pipeline_step(None, static_step(nsteps - 1), static_step(nsteps - 2), static_fin(nsteps - 2))
    pipeline_step(None, None, static_step(nsteps - 1), static_fin(nsteps - 1))
    pipeline_step(None, None, None, int(steps[0, nsteps - 1]))


def _attention(q, k, vt, bias, lq1, lk1, lq2, lk2, subln_g, *, batch, seq, lambda_init):
    n, d_qk = q.shape
    t = ATTN_TILE
    nq = seq // t
    nh = d_qk // V_HEAD_DIM
    hb = ATTN_HEADS
    aw = vt.shape[2]
    steps = _attn_steps(nq)
    assert steps.shape[1] >= 3 and nh % hb == 0
    vec = _resident((1, HEAD_DIM))
    head_rows = pl.BlockSpec((seq, hb * V_HEAD_DIM), lambda b, h: (b, h))
    return pl.pallas_call(
        functools.partial(_attn_kernel, lambda_init=lambda_init, steps=steps, heads=hb),
        grid=(batch, nh // hb),
        in_specs=[pl.BlockSpec(memory_space=pltpu.SMEM), head_rows, head_rows,
                  pl.BlockSpec((1, nq, hb * V_HEAD_DIM, t), lambda b, h: (b, 0, h, 0)),
                  pl.BlockSpec((hb, 2, t, t), lambda b, h: (h, 0, 0, 0)),
                  vec, vec, vec, vec, _resident((V_HEAD_DIM, 1))],
        out_specs=head_rows,
        out_shape=jax.ShapeDtypeStruct((n, aw), _F32),
        scratch_shapes=[pltpu.VMEM((hb, 2, t, t), _F32), pltpu.VMEM((hb, 2, t, t), _F32),
                        pltpu.VMEM((hb, 2, t, t), _BF16), pltpu.VMEM((hb, 2, t, t), _BF16),
                        pltpu.VMEM((2, hb, 2, 1, t), _F32), pltpu.VMEM((hb, 2, 1, t), _F32),
                        pltpu.VMEM((2, hb, 2, V_HEAD_DIM + _SUM_ROWS, t), _F32)],
        compiler_params=_params("parallel", "parallel"),
        name="diff_attn",
    )(jnp.asarray(steps), q, k, vt, bias, lq1, lk1, lq2, lk2, subln_g)


def _conv_kernel(c_ref, halo_ref, w_ref, cb_ref, lng_ref, lnb_ref, wpw_ref, bpw_ref, o_ref,
                 pad_ref, sh_ref, y_ref):
    rows = c_ref.shape[0]
    first = pl.program_id(1) == 0
    halo = halo_ref[...]
    pad_ref[:CONV_HALO] = jnp.where(first, jnp.zeros_like(halo), halo)
    pad_ref[CONV_HALO:] = c_ref[...]
    span = rows + CONV_HALO - 8
    for b in range(1, 8):
        sh_ref[b - 1, :span] = pad_ref[b:b + span]

    lead = CONV_HALO - (CONV_WIDTH - 1)

    def chunk(ci, carry):
        r0 = pl.multiple_of(ci * CONV_CHUNK, CONV_CHUNK)
        acc = jnp.zeros((CONV_CHUNK, c_ref.shape[1]), _F32)
        for kt in range(CONV_WIDTH):
            a, b = divmod(kt + lead, 8)
            if b == 0:
                xs = pad_ref[pl.ds(r0 + 8 * a, CONV_CHUNK), :]
            else:
                xs = sh_ref[b - 1, pl.ds(r0 + 8 * a, CONV_CHUNK), :]
            acc = acc + xs * w_ref[kt:kt + 1, :]
        y_ref[pl.ds(r0, CONV_CHUNK), :] = acc + cb_ref[...]
        return carry

    lax.fori_loop(0, rows // CONV_CHUNK, chunk, 0)

    y = y_ref[...]
    mu = jnp.mean(y, axis=-1, keepdims=True)
    yc = y - mu
    var = jnp.mean(yc * yc, axis=-1, keepdims=True)
    z = yc * lax.rsqrt(var + EPS) * lng_ref[...] + lnb_ref[...]
    z = z * jax.nn.sigmoid(z)
    o_ref[...] = _dot(z.astype(_BF16), wpw_ref[...]) + bpw_ref[...]


def _conv_branch(c, conv_w, conv_b, ln_g, ln_b, wpw, bpw, *, batch, seq):
    n, ch = c.shape
    d = wpw.shape[1]
    rows = CONV_ROWS
    nt = seq // rows
    halo_per_tile = rows // CONV_HALO
    return pl.pallas_call(
        _conv_kernel,
        grid=(batch, nt),
        in_specs=[pl.BlockSpec((rows, ch), lambda b, i: (b * nt + i, 0)),
                  pl.BlockSpec((CONV_HALO, ch),
                               lambda b, i: (jnp.maximum((b * nt + i) * halo_per_tile - 1, 0), 0)),
                  _resident(conv_w.shape), _resident((1, ch)), _resident((1, ch)), _resident((1, ch)),
                  _resident(wpw.shape), _resident((1, d))],
        out_specs=pl.BlockSpec((rows, d), lambda b, i: (b * nt + i, 0)),
        out_shape=jax.ShapeDtypeStruct((n, d), _F32),
        scratch_shapes=[pltpu.VMEM((rows + CONV_HALO, ch), _F32),
                        pltpu.VMEM((7, rows + CONV_HALO, ch), _F32),
                        pltpu.VMEM((rows, ch), _F32)],
        compiler_params=_params("parallel", "arbitrary"),
        name="conv_branch",
    )(c, c, conv_w, conv_b, ln_g, ln_b, wpw, bpw)


def _mlp_kernel(x_ref, attn_ref, conv_ref, gate_ref, wout_ref, g2_ref, wup_ref, wdown_ref, gf_ref,
                o_ref, *, final_norm):
    d = x_ref.shape[1]
    gates = gate_ref[...].astype(_F32)
    merged = gates[:, :d] * attn_ref[...] + gates[:, d:] * conv_ref[...]
    h1 = x_ref[...] + _dot(merged.astype(_BF16), wout_ref[...])
    u = (_rms(h1) * g2_ref[...]).astype(_BF16)
    dff = wup_ref.shape[1]
    acc = jnp.zeros(h1.shape, _F32)
    for c0 in range(0, dff, d):
        hid = jnp.maximum(_dot(u, wup_ref[:, c0:c0 + d]), 0.0)
        acc = acc + _dot((hid * hid).astype(_BF16), wdown_ref[c0:c0 + d, :])
    h2 = h1 + acc
    if final_norm:
        h2 = _rms(h2) * gf_ref[...]
    o_ref[...] = h2


def _merge_mlp(x2, attn, conv_out, gates, wout, g2, wup, wdown, gf, *, final_norm):
    n, d = x2.shape
    rows = MLP_ROWS
    row_spec = lambda w: pl.BlockSpec((rows, w), lambda i: (i, 0))
    return pl.pallas_call(
        functools.partial(_mlp_kernel, final_norm=final_norm),
        grid=(n // rows,),
        in_specs=[row_spec(d), row_spec(d), row_spec(d), row_spec(2 * d), _resident(wout.shape),
                  _resident((1, d)), _resident(wup.shape), _resident(wdown.shape), _resident((1, d))],
        out_specs=row_spec(d),
        out_shape=jax.ShapeDtypeStruct((n, d), _F32),
        compiler_params=_params("parallel"),
        name="merge_mlp",
    )(x2, attn, conv_out, gates, wout, g2, wup, wdown, gf)


def kernel(x, rel_bias, final_norm_g, norm_mix_g, w_in, b_glu, b_gate, lam_q1, lam_k1, lam_q2, lam_k2,
           subln_g, conv_w, conv_b, conv_ln_g, conv_ln_b, w_pw2, b_pw2, w_out, norm_mlp_g, w_up, w_down):
    batch, seq, d = x.shape
    depth = w_in.shape[0]
    ch = conv_w.shape[2]
    d_qk = (w_in.shape[2] - 2 * ch - 2 * d) // 3
    c1, c2, c3 = 2 * d_qk, 3 * d_qk, 3 * d_qk + 2 * ch
    assert seq % ATTN_TILE == 0 and seq % PROJ_ROWS == 0 and PROJ_ROWS % ATTN_TILE == 0
    assert seq % CONV_ROWS == 0 and (batch * seq) % MLP_ROWS == 0 and ATTN_TILE >= MAX_DISTANCE

    row = lambda v: v.reshape(1, -1)
    h = x.reshape(batch * seq, d)
    bias = _bias_tiles(rel_bias)
    for l in range(depth):
        lambda_init = 0.8 - 0.6 * math.exp(-0.3 * l)
        w = w_in[l]
        q, k, vt, c, gates = _proj(
            h, row(norm_mix_g[l]), w[:, :c1].astype(_BF16), w[:, c1:c2].T.astype(_BF16),
            w[:, c2:c3].astype(_BF16), w[:, c3:].astype(_BF16), row(b_glu[l]), row(b_gate[l]),
            batch=batch, seq=seq)
        attn = _attention(q, k, vt, bias, row(lam_q1[l]), row(lam_k1[l]), row(lam_q2[l]), row(lam_k2[l]),
                          subln_g[l].reshape(-1, 1), batch=batch, seq=seq, lambda_init=lambda_init)
        conv_out = _conv_branch(c, conv_w[l], row(conv_b[l]), row(conv_ln_g[l]), row(conv_ln_b[l]),
                                w_pw2[l].astype(_BF16), row(b_pw2[l]), batch=batch, seq=seq)
        h = _merge_mlp(h, attn, conv_out, gates, w_out[l].astype(_BF16), row(norm_mlp_g[l]),
                       w_up[l].astype(_BF16), w_down[l].astype(_BF16), row(final_norm_g),
                       final_norm=(l == depth - 1))
    return h.reshape(batch, seq, d)
```

```python
import functools
import math

import jax
import jax.numpy as jnp
import numpy as np
from jax import lax
from jax.experimental import pallas as pl
from jax.experimental.pallas import tpu as pltpu

HEAD_DIM = 64
V_HEAD_DIM = 2 * HEAD_DIM
CONV_WIDTH = 31
NUM_BUCKETS = 32
MAX_DISTANCE = 128
EPS = 1e-6
MASKED = -1e30
LOG2E = 1.4426950408889634

PROJ_ROWS = 512
ATTN_TILE = 512
ATTN_COLS = 256
ATTN_HEADS = 2
MLP_ROWS = 512
CONV_SUB = 128
CONV_HALO = 32
CONV_CHUNK = 16
SUBLANES = 8
VMEM_LIMIT = 56 * 1024 * 1024

_BF16 = jnp.bfloat16
_F32 = jnp.float32


def _resident(shape):
    return pl.BlockSpec(shape, lambda *_: (0,) * len(shape), pipeline_mode=pl.Buffered(1))


def _params(*semantics):
    return pltpu.CompilerParams(dimension_semantics=semantics, vmem_limit_bytes=VMEM_LIMIT)


def _rms(xf):
    return xf * lax.rsqrt(jnp.mean(xf * xf, axis=-1, keepdims=True) + EPS)


def _dot(a, b):
    return jnp.dot(a, b, preferred_element_type=_F32)


def _dot_nt(a, b):
    return lax.dot_general(a, b, (((1,), (1,)), ((), ())), preferred_element_type=_F32)


def _proj_kernel(x_ref, g_ref, wqk_ref, wvt_ref, wglu_ref, wgate_ref, bglu_ref, bgate_ref,
                 q_ref, k_ref, vt_ref, c_ref, gate_ref, *, d_qk, kv_tiles_per_block):
    u = (_rms(x_ref[...]) * g_ref[...]).astype(_BF16)
    scale = LOG2E / math.sqrt(HEAD_DIM)
    q_ref[...] = (_dot(u, wqk_ref[:, :d_qk]) * scale).astype(_BF16)
    k_ref[...] = _dot(u, wqk_ref[:, d_qk:]).astype(_BF16)
    vt = _dot_nt(wvt_ref[...], u).astype(_BF16)
    for t in range(kv_tiles_per_block):
        vt_ref[0, t] = vt[:, t * ATTN_TILE:(t + 1) * ATTN_TILE]
    ch = c_ref.shape[-1]
    ga = _dot(u, wglu_ref[:, :ch]) + bglu_ref[:, :ch]
    gb = _dot(u, wglu_ref[:, ch:]) + bglu_ref[:, ch:]
    c_ref[...] = ga * jax.nn.sigmoid(gb)
    gate_ref[...] = jax.nn.sigmoid(_dot(u, wgate_ref[...]) + bgate_ref[...]).astype(_BF16)


def _proj(x2, g, wqk, wvt, wglu, wgate, bglu, bgate, *, batch, seq):
    n, d = x2.shape
    d_qk = wqk.shape[1] // 2
    aw = wvt.shape[0]
    ch = wglu.shape[1] // 2
    ngate = wgate.shape[1]
    rows = PROJ_ROWS
    tiles = rows // ATTN_TILE
    blocks_per_seq = seq // rows
    row_spec = lambda w: pl.BlockSpec((rows, w), lambda i: (i, 0))
    return pl.pallas_call(
        functools.partial(_proj_kernel, d_qk=d_qk, kv_tiles_per_block=tiles),
        grid=(n // rows,),
        in_specs=[row_spec(d), _resident((1, d)), _resident(wqk.shape), _resident(wvt.shape),
                  _resident(wglu.shape), _resident(wgate.shape), _resident(bglu.shape),
                  _resident(bgate.shape)],
        out_specs=[row_spec(d_qk), row_spec(d_qk),
                   pl.BlockSpec((1, tiles, aw, ATTN_TILE),
                                lambda i: (i // blocks_per_seq, i % blocks_per_seq, 0, 0)),
                   row_spec(ch), row_spec(ngate)],
        out_shape=[jax.ShapeDtypeStruct((n, d_qk), _BF16), jax.ShapeDtypeStruct((n, d_qk), _BF16),
                   jax.ShapeDtypeStruct((batch, seq // ATTN_TILE, aw, ATTN_TILE), _BF16),
                   jax.ShapeDtypeStruct((n, ch), _F32), jax.ShapeDtypeStruct((n, ngate), _BF16)],
        compiler_params=_params("parallel"),
        name="proj",
    )(x2, g, wqk, wvt, wglu, wgate, bglu, bgate)


def _bucket_tiles():
    t = ATTN_TILE
    r = np.arange(t, dtype=np.int64)[:, None]
    c = np.arange(t, dtype=np.int64)[None, :]

    def bucket(rel):
        n = np.maximum(rel, 0)
        max_exact = NUM_BUCKETS // 2
        nf = np.maximum(n, 1).astype(np.float32)
        large = max_exact + (np.log(nf / np.float32(max_exact)) / np.float32(math.log(MAX_DISTANCE / max_exact))
                             * np.float32(NUM_BUCKETS - max_exact)).astype(np.int32)
        large = np.minimum(large, NUM_BUCKETS - 1)
        return np.where(n < max_exact, n, large).astype(np.int32)

    diag = np.where(c - r >= 0, bucket(c - r), -1)
    left = bucket(t + c - r)
    return np.stack([diag, left]).astype(np.int32)


def _bias_kernel(rb_ref, bucket_ref, out_ref):
    h = pl.program_id(0)
    bucket = bucket_ref[0]
    far = rb_ref[NUM_BUCKETS - 1, h]
    acc = jnp.full(bucket.shape, MASKED, _F32)
    for b in range(NUM_BUCKETS):
        acc = jnp.where(bucket == b, (rb_ref[b, h] - far) * LOG2E, acc)
    out_ref[0, 0] = acc


def _bias_tiles(rel_bias):
    nh = rel_bias.shape[1]
    t = ATTN_TILE
    buckets = jnp.asarray(_bucket_tiles())
    return pl.pallas_call(
        _bias_kernel,
        grid=(nh, 2),
        in_specs=[pl.BlockSpec(memory_space=pltpu.SMEM),
                  pl.BlockSpec((1, t, t), lambda h, j: (j, 0, 0))],
        out_specs=pl.BlockSpec((1, 1, t, t), lambda h, j: (h, j, 0, 0)),
        out_shape=jax.ShapeDtypeStruct((nh, 2, t, t), _F32),
        compiler_params=_params("parallel", "parallel"),
        name="bias_tiles",
    )(rel_bias, buckets)


MAX_CHAINS = 8
_PLAIN, _DIAG = 0, 1
_SUM_ROWS = 16


def _attn_steps(nq):
    steps = [[i, j, _DIAG if j == i else _PLAIN, 0] for i in range(nq) for j in range(i + 1)]
    for n in range(1, len(steps)):
        steps[n][3] = int(steps[n - 1][2] == _DIAG)
    return np.asarray(steps, dtype=np.int32).T.copy()


def _col_max(s):
    groups = [s[r:r + SUBLANES] for r in range(0, s.shape[0], SUBLANES)]
    acc = groups[:MAX_CHAINS]
    for n, g in enumerate(groups[MAX_CHAINS:]):
        acc[n % MAX_CHAINS] = jnp.maximum(acc[n % MAX_CHAINS], g)
    while len(acc) > 1:
        acc = [jnp.maximum(acc[n], acc[n + 1]) for n in range(0, len(acc), 2)]
    return jnp.max(acc[0], axis=0, keepdims=True)


def _tile_rows(idx, offset=0, size=ATTN_TILE):
    start = idx * ATTN_TILE + offset
    if not isinstance(start, int):
        start = pl.multiple_of(start, math.gcd(ATTN_TILE, offset))
    return pl.ds(start, size)


def _attn_kernel(tab_ref, q_ref, k_ref, vt_ref, bias_ref, lq1_ref, lk1_ref, lq2_ref, lk2_ref, g_ref,
                 o_ref, s0_ref, s1_ref, p0_ref, p1_ref, alpha_ref, m_ref, acc_ref, *, lambda_init, steps, heads):
    t, w = ATTN_TILE, ATTN_COLS
    s_ref = (s0_ref, s1_ref)
    p_ref = (p0_ref, p1_ref)
    nsteps = steps.shape[1]
    parts = [(hh, a, c) for hh in range(heads) for a in range(2) for c in range(0, t, w)]

    def head_cols(hh):
        return slice(hh * V_HEAD_DIM, (hh + 1) * V_HEAD_DIM)

    def scores(i, j, slot, hh, a, c):
        q = q_ref[_tile_rows(i, c, w), head_cols(hh)]
        kk = k_ref[_tile_rows(j), head_cols(hh)]
        lane = lax.broadcasted_iota(jnp.int32, q.shape, 1)
        keep = lane < HEAD_DIM if a == 0 else lane >= HEAD_DIM
        s_ref[slot][hh, a, :, c:c + w] = _dot_nt(kk, jnp.where(keep, q, jnp.zeros_like(q)))

    def softmax(i, j, slot, kind, hh, a, c):
        if kind == _PLAIN:
            if c == 0:
                near = jnp.where(j == i - 1, 1.0, 0.0).astype(_F32)
                s_ref[slot][hh, a, t - MAX_DISTANCE:, :MAX_DISTANCE] += (
                    bias_ref[hh, 1, t - MAX_DISTANCE:, :MAX_DISTANCE] * near)
            s = s_ref[slot][hh, a, :, c:c + w]
        else:
            s = s_ref[slot][hh, a, :, c:c + w] + bias_ref[hh, 0, :, c:c + w]
        m_old = jnp.where(j == 0, -jnp.inf, m_ref[hh, a, :, c:c + w])
        m_new = jnp.maximum(m_old, _col_max(s))
        alpha_ref[slot, hh, a, :, c:c + w] = jnp.exp2(m_old - m_new)
        p_ref[slot][hh, a, :, c:c + w] = jnp.exp2(s - m_new).astype(_BF16)
        m_ref[hh, a, :, c:c + w] = m_new

    def values(i, j, slot, hh, a, c):
        vt = jnp.concatenate([vt_ref[0, j, head_cols(hh), :], jnp.ones((_SUM_ROWS, t), _BF16)], axis=0)
        acc_ref[i % 2, hh, a, :, c:c + w] = (alpha_ref[slot, hh, a, :, c:c + w] * acc_ref[i % 2, hh, a, :, c:c + w]
                                             + _dot(vt, p_ref[slot][hh, a, :, c:c + w]))

    def finalize(i):
        lam = (jnp.exp(jnp.sum(lq1_ref[...] * lk1_ref[...])) - jnp.exp(jnp.sum(lq2_ref[...] * lk2_ref[...]))
               + lambda_init)
        for hh in range(heads):
            a1 = acc_ref[i % 2, hh, 0]
            a2 = acc_ref[i % 2, hh, 1]
            r1 = 1.0 / a1[V_HEAD_DIM:V_HEAD_DIM + 1]
            r2 = lam / a2[V_HEAD_DIM:V_HEAD_DIM + 1]
            o = a1[:V_HEAD_DIM] * r1 - a2[:V_HEAD_DIM] * r2
            o = o * lax.rsqrt(jnp.mean(o * o, axis=0, keepdims=True) + EPS)
            o = o * (g_ref[...] * (1.0 - lambda_init))
            o_ref[_tile_rows(i), head_cols(hh)] = o.T

    def pipeline_step(sc, sm, pv, fin):
        if fin is not None:
            finalize(fin)
        for hh, a, c in parts:
            if sc is not None:
                scores(sc[0], sc[1], sc[3], hh, a, c)
            if pv is not None:
                values(pv[0], pv[1], pv[3], hh, a, c)
            if sm is not None:
                softmax(sm[0], sm[1], sm[3], sm[2], hh, a, c)

    def static_step(n):
        return int(steps[0, n]), int(steps[1, n]), int(steps[2, n]), n % 2

    def static_fin(n):
        return int(steps[0, n - 1]) if n >= 1 and steps[3, n] else None

    acc_ref[...] = jnp.zeros(acc_ref.shape, _F32)
    m_ref[...] = jnp.full(m_ref.shape, -jnp.inf, _F32)

    pipeline_step(static_step(0), None, None, None)
    pipeline_step(static_step(1), static_step(0), None, None)

    full = nsteps - 2
    combos = sorted({(int(steps[2, n + 1]), int(steps[3, n])) for n in range(full)})

    def iteration(n, par):
        k_sm = tab_ref[2, n + 1]
        k_fin = tab_ref[3, n]
        for c_sm, c_fin in combos:
            @pl.when((k_sm == c_sm) & (k_fin == c_fin))
            def _():
                pipeline_step((tab_ref[0, n + 2], tab_ref[1, n + 2], None, par),
                              (tab_ref[0, n + 1], tab_ref[1, n + 1], c_sm, 1 - par),
                              (tab_ref[0, n], tab_ref[1, n], None, par),
                              tab_ref[0, jnp.maximum(n - 1, 0)] if c_fin else None)

    def body(pair, carry):
        iteration(2 * pair, 0)
        iteration(2 * pair + 1, 1)
        return carry

    lax.fori_loop(0, full // 2, body, 0)
    if full % 2:
        iteration(full - 1, (full - 1) % 2)

    pipeline_step(None, static_step(nsteps - 1), static_step(nsteps - 2), static_fin(nsteps - 2))
    pipeline_step(None, None, static_step(nsteps - 1), static_fin(nsteps - 1))
    pipeline_step(None, None, None, int(steps[0, nsteps - 1]))


def _attention(q, k, vt, bias, lq1, lk1, lq2, lk2, subln_g, *, batch, seq, lambda_init):
    n, d_qk = q.shape
    t = ATTN_TILE
    nq = seq // t
    nh = d_qk // V_HEAD_DIM
    hb = ATTN_HEADS
    aw = vt.shape[2]
    steps = _attn_steps(nq)
    assert steps.shape[1] >= 3 and nh % hb == 0
    vec = _resident((1, HEAD_DIM))
    head_rows = pl.BlockSpec((seq, hb * V_HEAD_DIM), lambda b, h: (b, h))
    return pl.pallas_call(
        functools.partial(_attn_kernel, lambda_init=lambda_init, steps=steps, heads=hb),
        grid=(batch, nh // hb),
        in_specs=[pl.BlockSpec(memory_space=pltpu.SMEM), head_rows, head_rows,
                  pl.BlockSpec((1, nq, hb * V_HEAD_DIM, t), lambda b, h: (b, 0, h, 0)),
                  pl.BlockSpec((hb, 2, t, t), lambda b, h: (h, 0, 0, 0)),
                  vec, vec, vec, vec, _resident((V_HEAD_DIM, 1))],
        out_specs=head_rows,
        out_shape=jax.ShapeDtypeStruct((n, aw), _F32),
        scratch_shapes=[pltpu.VMEM((hb, 2, t, t), _F32), pltpu.VMEM((hb, 2, t, t), _F32),
                        pltpu.VMEM((hb, 2, t, t), _BF16), pltpu.VMEM((hb, 2, t, t), _BF16),
                        pltpu.VMEM((2, hb, 2, 1, t), _F32), pltpu.VMEM((hb, 2, 1, t), _F32),
                        pltpu.VMEM((2, hb, 2, V_HEAD_DIM + _SUM_ROWS, t), _F32)],
        compiler_params=_params("parallel", "parallel"),
        name="diff_attn",
    )(jnp.asarray(steps), q, k, vt, bias, lq1, lk1, lq2, lk2, subln_g)


def _zero_after(value, zero_ref):
    bits = lax.bitcast_convert_type(value, jnp.uint32) & zero_ref[...]
    return lax.bitcast_convert_type(bits, _F32)


def _branch_mlp_kernel(x_ref, attn_ref, gate_ref, c_ref, halo_ref, wb_ref, zero_ref, lng_ref, lnb_ref,
                       wpw_ref, bpw_ref, wout_ref, g2_ref, wup_ref, wdown_ref, gf_ref, o_ref,
                       y_ref, pad_ref, sh_ref, *, final_norm, tiles_per_seq):
    s = pl.program_id(0)
    rows, d = x_ref.shape
    first_of_seq = s % tiles_per_seq == 0
    lead = CONV_HALO - (CONV_WIDTH - 1)
    span = CONV_SUB + CONV_HALO - SUBLANES
    groups = CONV_CHUNK // SUBLANES
    nchunks = rows // CONV_CHUNK
    dff = wup_ref.shape[1]
    ndots = 2 + 2 * (dff // d)

    @pl.when(s == 0)
    def _():
        y_ref[...] = jnp.zeros(y_ref.shape, _F32)

    def conv_chunk(k, zero):
        sub, r0 = divmod(k * CONV_CHUNK, CONV_SUB)
        base = sub * CONV_SUB
        if r0 == 0:
            if sub == 0:
                halo = halo_ref[...]
                pad_ref[:CONV_HALO] = jnp.where(first_of_seq, jnp.zeros_like(halo), halo)
            else:
                pad_ref[:CONV_HALO] = c_ref[base - CONV_HALO:base]
            body = c_ref[base:base + CONV_SUB]
            if zero is not None:
                body = (body.reshape(CONV_SUB // SUBLANES, SUBLANES, -1) + zero[None]).reshape(body.shape)
            pad_ref[CONV_HALO:] = body
            for b in range(1, SUBLANES):
                sh_ref[b - 1, :span] = pad_ref[b:b + span]
        bias = wb_ref[CONV_WIDTH] if zero is None else wb_ref[CONV_WIDTH] + zero
        acc = [bias] * groups
        for kt in range(CONV_WIDTH):
            a, b = divmod(kt + lead, SUBLANES)
            wv = wb_ref[kt]
            for g in range(groups):
                r = r0 + SUBLANES * (a + g)
                xs = pad_ref[r:r + SUBLANES] if b == 0 else sh_ref[b - 1, r:r + SUBLANES]
                acc[g] = acc[g] + xs * wv
        for g in range(groups):
            r = base + r0 + SUBLANES * g
            y_ref[r:r + SUBLANES] = acc[g]

    anchors = {}
    for k in range(1, nchunks):
        pos = (k - 1) * ndots / nchunks
        dot_idx = int(pos)
        row = int((pos - dot_idx) * rows) // SUBLANES * SUBLANES
        anchors.setdefault(dot_idx, []).append((k, row))

    def after_dot(dot_idx, result):
        for k, row in anchors.get(dot_idx, []):
            conv_chunk(k, _zero_after(result[row:row + SUBLANES], zero_ref))

    y = y_ref[...]
    mu = jnp.mean(y, axis=-1, keepdims=True)
    yc = y - mu
    var = jnp.mean(yc * yc, axis=-1, keepdims=True)
    z = yc * lax.rsqrt(var + EPS) * lng_ref[...] + lnb_ref[...]
    z = (z * jax.nn.sigmoid(z)).astype(_BF16)
    conv_chunk(0, None)

    pw = _dot(z, wpw_ref[...])
    after_dot(0, pw)
    gates = gate_ref[...].astype(_F32)
    merged = gates[:, :d] * attn_ref[...] + gates[:, d:] * (pw + bpw_ref[...])
    out = _dot(merged.astype(_BF16), wout_ref[...])
    after_dot(1, out)
    h1 = x_ref[...] + out
    u = (_rms(h1) * g2_ref[...]).astype(_BF16)
    acc = jnp.zeros(h1.shape, _F32)
    for n, c0 in enumerate(range(0, dff, d)):
        up = _dot(u, wup_ref[:, c0:c0 + d])
        after_dot(2 + 2 * n, up)
        hid = jnp.maximum(up, 0.0)
        down = _dot((hid * hid).astype(_BF16), wdown_ref[c0:c0 + d, :])
        after_dot(3 + 2 * n, down)
        acc = acc + down
    h2 = h1 + acc
    if final_norm:
        h2 = _rms(h2) * gf_ref[...]
    o_ref[...] = h2


def _branch_mlp(x2, attn, gates, c, wb, ln_g, ln_b, wpw, bpw, wout, g2, wup, wdown, gf, *, seq, final_norm):
    n, d = x2.shape
    ch = c.shape[1]
    rows = MLP_ROWS
    nt = n // rows
    halo_per_tile = rows // CONV_HALO
    prev = lambda w: pl.BlockSpec((rows, w), lambda s: (jnp.maximum(s - 1, 0), 0))
    cur = lambda s: jnp.minimum(s, nt - 1)
    return pl.pallas_call(
        functools.partial(_branch_mlp_kernel, final_norm=final_norm, tiles_per_seq=seq // rows),
        grid=(nt + 1,),
        in_specs=[prev(d), prev(d), prev(2 * d),
                  pl.BlockSpec((rows, ch), lambda s: (cur(s), 0)),
                  pl.BlockSpec((CONV_HALO, ch), lambda s: (jnp.maximum(cur(s) * halo_per_tile - 1, 0), 0)),
                  _resident(wb.shape), _resident((SUBLANES, ch)), _resident((1, ch)), _resident((1, ch)),
                  _resident(wpw.shape),
                  _resident((1, d)), _resident(wout.shape), _resident((1, d)), _resident(wup.shape),
                  _resident(wdown.shape), _resident((1, d))],
        out_specs=prev(d),
        out_shape=jax.ShapeDtypeStruct((n, d), _F32),
        scratch_shapes=[pltpu.VMEM((rows, ch), _F32),
                        pltpu.VMEM((CONV_SUB + CONV_HALO, ch), _F32),
                        pltpu.VMEM((SUBLANES - 1, CONV_SUB + CONV_HALO, ch), _F32)],
        compiler_params=_params("arbitrary"),
        name="branch_mlp",
    )(x2, attn, gates, c, c, wb, jnp.zeros((SUBLANES, ch), jnp.uint32), ln_g, ln_b, wpw, bpw, wout, g2, wup,
      wdown, gf)


def kernel(x, rel_bias, final_norm_g, norm_mix_g, w_in, b_glu, b_gate, lam_q1, lam_k1, lam_q2, lam_k2,
           subln_g, conv_w, conv_b, conv_ln_g, conv_ln_b, w_pw2, b_pw2, w_out, norm_mlp_g, w_up, w_down):
    batch, seq, d = x.shape
    depth = w_in.shape[0]
    ch = conv_w.shape[2]
    d_qk = (w_in.shape[2] - 2 * ch - 2 * d) // 3
    c1, c2, c3 = 2 * d_qk, 3 * d_qk, 3 * d_qk + 2 * ch
    assert seq % ATTN_TILE == 0 and seq % PROJ_ROWS == 0 and PROJ_ROWS % ATTN_TILE == 0
    assert seq % MLP_ROWS == 0 and MLP_ROWS % CONV_SUB == 0 and ATTN_TILE >= MAX_DISTANCE

    row = lambda v: v.reshape(1, -1)
    h = x.reshape(batch * seq, d)
    bias = _bias_tiles(rel_bias)
    for l in range(depth):
        lambda_init = 0.8 - 0.6 * math.exp(-0.3 * l)
        w = w_in[l]
        q, k, vt, c, gates = _proj(
            h, row(norm_mix_g[l]), w[:, :c1].astype(_BF16), w[:, c1:c2].T.astype(_BF16),
            w[:, c2:c3].astype(_BF16), w[:, c3:].astype(_BF16), row(b_glu[l]), row(b_gate[l]),
            batch=batch, seq=seq)
        attn = _attention(q, k, vt, bias, row(lam_q1[l]), row(lam_k1[l]), row(lam_q2[l]), row(lam_k2[l]),
                          subln_g[l].reshape(-1, 1), batch=batch, seq=seq, lambda_init=lambda_init)
        wb = jnp.concatenate([conv_w[l], conv_b[l][None]], axis=0)
        wb = jnp.broadcast_to(wb[:, None, :], (CONV_WIDTH + 1, SUBLANES, ch))
        h = _branch_mlp(h, attn, gates, c, wb, row(conv_ln_g[l]), row(conv_ln_b[l]), w_pw2[l].astype(_BF16),
                        row(b_pw2[l]), w_out[l].astype(_BF16), row(norm_mlp_g[l]), w_up[l].astype(_BF16),
                        w_down[l].astype(_BF16), row(final_norm_g), seq=seq, final_norm=(l == depth - 1))
    return h.reshape(batch, seq, d)
```

```python
import functools
import math

import jax
import jax.numpy as jnp
import numpy as np
from jax import lax
from jax.experimental import pallas as pl
from jax.experimental.pallas import tpu as pltpu

HEAD_DIM = 64
V_HEAD_DIM = 2 * HEAD_DIM
CONV_WIDTH = 31
NUM_BUCKETS = 32
MAX_DISTANCE = 128
EPS = 1e-6
MASKED = -1e30
LOG2E = 1.4426950408889634

PROJ_ROWS = 512
ATTN_TILE = 512
ATTN_COLS = 256
ATTN_HEADS = 2
MLP_ROWS = 512
CONV_SUB = 128
CONV_HALO = 32
CONV_CHUNK = 16
CONV_LANES = 512
SUBLANES = 8
VMEM_LIMIT = 56 * 1024 * 1024

_BF16 = jnp.bfloat16
_F32 = jnp.float32


def _resident(shape):
    return pl.BlockSpec(shape, lambda *_: (0,) * len(shape), pipeline_mode=pl.Buffered(1))


def _params(*semantics):
    return pltpu.CompilerParams(dimension_semantics=semantics, vmem_limit_bytes=VMEM_LIMIT)


def _rms(xf):
    return xf * lax.rsqrt(jnp.mean(xf * xf, axis=-1, keepdims=True) + EPS)


def _dot(a, b):
    return jnp.dot(a, b, preferred_element_type=_F32)


def _dot_nt(a, b):
    return lax.dot_general(a, b, (((1,), (1,)), ((), ())), preferred_element_type=_F32)


def _proj_kernel(x_ref, g_ref, wqk_ref, wvt_ref, wglu_ref, wgate_ref, bglu_ref, bgate_ref,
                 q_ref, k_ref, vt_ref, c_ref, gate_ref, *, d_qk, kv_tiles_per_block):
    u = (_rms(x_ref[...]) * g_ref[...]).astype(_BF16)
    scale = LOG2E / math.sqrt(HEAD_DIM)
    q_ref[...] = (_dot(u, wqk_ref[:, :d_qk]) * scale).astype(_BF16)
    k_ref[...] = _dot(u, wqk_ref[:, d_qk:]).astype(_BF16)
    vt = _dot_nt(wvt_ref[...], u).astype(_BF16)
    for t in range(kv_tiles_per_block):
        vt_ref[0, t] = vt[:, t * ATTN_TILE:(t + 1) * ATTN_TILE]
    ch = c_ref.shape[-1]
    ga = _dot(u, wglu_ref[:, :ch]) + bglu_ref[:, :ch]
    gb = _dot(u, wglu_ref[:, ch:]) + bglu_ref[:, ch:]
    c_ref[...] = ga * jax.nn.sigmoid(gb)
    gate_ref[...] = jax.nn.sigmoid(_dot(u, wgate_ref[...]) + bgate_ref[...]).astype(_BF16)


def _proj(x2, g, wqk, wvt, wglu, wgate, bglu, bgate, *, batch, seq):
    n, d = x2.shape
    d_qk = wqk.shape[1] // 2
    aw = wvt.shape[0]
    ch = wglu.shape[1] // 2
    ngate = wgate.shape[1]
    rows = PROJ_ROWS
    tiles = rows // ATTN_TILE
    blocks_per_seq = seq // rows
    row_spec = lambda w: pl.BlockSpec((rows, w), lambda i: (i, 0))
    return pl.pallas_call(
        functools.partial(_proj_kernel, d_qk=d_qk, kv_tiles_per_block=tiles),
        grid=(n // rows,),
        in_specs=[row_spec(d), _resident((1, d)), _resident(wqk.shape), _resident(wvt.shape),
                  _resident(wglu.shape), _resident(wgate.shape), _resident(bglu.shape),
                  _resident(bgate.shape)],
        out_specs=[row_spec(d_qk), row_spec(d_qk),
                   pl.BlockSpec((1, tiles, aw, ATTN_TILE),
                                lambda i: (i // blocks_per_seq, i % blocks_per_seq, 0, 0)),
                   row_spec(ch), row_spec(ngate)],
        out_shape=[jax.ShapeDtypeStruct((n, d_qk), _BF16), jax.ShapeDtypeStruct((n, d_qk), _BF16),
                   jax.ShapeDtypeStruct((batch, seq // ATTN_TILE, aw, ATTN_TILE), _BF16),
                   jax.ShapeDtypeStruct((n, ch), _F32), jax.ShapeDtypeStruct((n, ngate), _BF16)],
        compiler_params=_params("parallel"),
        name="proj",
    )(x2, g, wqk, wvt, wglu, wgate, bglu, bgate)


def _bucket_tiles():
    t = ATTN_TILE
    r = np.arange(t, dtype=np.int64)[:, None]
    c = np.arange(t, dtype=np.int64)[None, :]

    def bucket(rel):
        n = np.maximum(rel, 0)
        max_exact = NUM_BUCKETS // 2
        nf = np.maximum(n, 1).astype(np.float32)
        large = max_exact + (np.log(nf / np.float32(max_exact)) / np.float32(math.log(MAX_DISTANCE / max_exact))
                             * np.float32(NUM_BUCKETS - max_exact)).astype(np.int32)
        large = np.minimum(large, NUM_BUCKETS - 1)
        return np.where(n < max_exact, n, large).astype(np.int32)

    diag = np.where(c - r >= 0, bucket(c - r), -1)
    left = bucket(t + c - r)
    return np.stack([diag, left]).astype(np.int32)


def _bias_kernel(rb_ref, bucket_ref, out_ref):
    h = pl.program_id(0)
    bucket = bucket_ref[0]
    far = rb_ref[NUM_BUCKETS - 1, h]
    acc = jnp.full(bucket.shape, MASKED, _F32)
    for b in range(NUM_BUCKETS):
        acc = jnp.where(bucket == b, (rb_ref[b, h] - far) * LOG2E, acc)
    out_ref[0, 0] = acc


def _bias_tiles(rel_bias):
    nh = rel_bias.shape[1]
    t = ATTN_TILE
    buckets = jnp.asarray(_bucket_tiles())
    return pl.pallas_call(
        _bias_kernel,
        grid=(nh, 2),
        in_specs=[pl.BlockSpec(memory_space=pltpu.SMEM),
                  pl.BlockSpec((1, t, t), lambda h, j: (j, 0, 0))],
        out_specs=pl.BlockSpec((1, 1, t, t), lambda h, j: (h, j, 0, 0)),
        out_shape=jax.ShapeDtypeStruct((nh, 2, t, t), _F32),
        compiler_params=_params("parallel", "parallel"),
        name="bias_tiles",
    )(rel_bias, buckets)


MAX_CHAINS = 8
_PLAIN, _DIAG = 0, 1
_SUM_ROWS = 16


def _attn_steps(nq):
    steps = [[i, j, _DIAG if j == i else _PLAIN, 0] for i in range(nq) for j in range(i + 1)]
    for n in range(1, len(steps)):
        steps[n][3] = int(steps[n - 1][2] == _DIAG)
    return np.asarray(steps, dtype=np.int32).T.copy()


def _col_max(s):
    groups = [s[r:r + SUBLANES] for r in range(0, s.shape[0], SUBLANES)]
    acc = groups[:MAX_CHAINS]
    for n, g in enumerate(groups[MAX_CHAINS:]):
        acc[n % MAX_CHAINS] = jnp.maximum(acc[n % MAX_CHAINS], g)
    while len(acc) > 1:
        acc = [jnp.maximum(acc[n], acc[n + 1]) for n in range(0, len(acc), 2)]
    return jnp.max(acc[0], axis=0, keepdims=True)


def _tile_rows(idx, offset=0, size=ATTN_TILE):
    start = idx * ATTN_TILE + offset
    if not isinstance(start, int):
        start = pl.multiple_of(start, math.gcd(ATTN_TILE, offset))
    return pl.ds(start, size)


def _attn_kernel(tab_ref, q_ref, k_ref, vt_ref, bias_ref, lq1_ref, lk1_ref, lq2_ref, lk2_ref, g_ref,
                 o_ref, s0_ref, s1_ref, p0_ref, p1_ref, alpha_ref, m_ref, acc_ref, *, lambda_init, steps, heads):
    t, w = ATTN_TILE, ATTN_COLS
    s_ref = (s0_ref, s1_ref)
    p_ref = (p0_ref, p1_ref)
    nsteps = steps.shape[1]
    parts = [(hh, a, c) for hh in range(heads) for a in range(2) for c in range(0, t, w)]

    def head_cols(hh):
        return slice(hh * V_HEAD_DIM, (hh + 1) * V_HEAD_DIM)

    def scores(i, j, slot, hh, a, c):
        q = q_ref[_tile_rows(i, c, w), head_cols(hh)]
        kk = k_ref[_tile_rows(j), head_cols(hh)]
        lane = lax.broadcasted_iota(jnp.int32, q.shape, 1)
        keep = lane < HEAD_DIM if a == 0 else lane >= HEAD_DIM
        s_ref[slot][hh, a, c // w] = _dot_nt(kk, jnp.where(keep, q, jnp.zeros_like(q)))

    def softmax(i, j, slot, kind, hh, a, c):
        cb = c // w
        if kind == _PLAIN:
            if c == 0:
                near = jnp.where(j == i - 1, 1.0, 0.0).astype(_F32)
                s_ref[slot][hh, a, cb, t - MAX_DISTANCE:, :MAX_DISTANCE] += (
                    bias_ref[hh, 1, t - MAX_DISTANCE:, :MAX_DISTANCE] * near)
            s = s_ref[slot][hh, a, cb]
        else:
            s = s_ref[slot][hh, a, cb] + bias_ref[hh, 0, :, c:c + w]
        m_old = jnp.where(j == 0, -jnp.inf, m_ref[hh, a, cb])
        m_new = jnp.maximum(m_old, _col_max(s))
        alpha_ref[slot, hh, a, cb] = jnp.exp2(m_old - m_new)
        p_ref[slot][hh, a, cb] = jnp.exp2(s - m_new).astype(_BF16)
        m_ref[hh, a, cb] = m_new

    def values(i, j, slot, hh, a, c):
        cb = c // w
        vt = jnp.concatenate([vt_ref[0, j, head_cols(hh), :], jnp.ones((_SUM_ROWS, t), _BF16)], axis=0)
        acc_ref[i % 2, hh, a, cb] = (alpha_ref[slot, hh, a, cb] * acc_ref[i % 2, hh, a, cb]
                                     + _dot(vt, p_ref[slot][hh, a, cb]))

    def finalize(i):
        lam = (jnp.exp(jnp.sum(lq1_ref[...] * lk1_ref[...])) - jnp.exp(jnp.sum(lq2_ref[...] * lk2_ref[...]))
               + lambda_init)
        for hh in range(heads):
            for cb in range(t // w):
                a1 = acc_ref[i % 2, hh, 0, cb]
                a2 = acc_ref[i % 2, hh, 1, cb]
                r1 = 1.0 / a1[V_HEAD_DIM:V_HEAD_DIM + 1]
                r2 = lam / a2[V_HEAD_DIM:V_HEAD_DIM + 1]
                o = a1[:V_HEAD_DIM] * r1 - a2[:V_HEAD_DIM] * r2
                o = o * lax.rsqrt(jnp.mean(o * o, axis=0, keepdims=True) + EPS)
                o = o * (g_ref[...] * (1.0 - lambda_init))
                o_ref[_tile_rows(i, cb * w, w), head_cols(hh)] = o.T

    def pipeline_step(sc, sm, pv, fin):
        if fin is not None:
            finalize(fin)
        for hh, a, c in parts:
            if sc is not None:
                scores(sc[0], sc[1], sc[3], hh, a, c)
            if pv is not None:
                values(pv[0], pv[1], pv[3], hh, a, c)
            if sm is not None:
                softmax(sm[0], sm[1], sm[3], sm[2], hh, a, c)

    def static_step(n):
        return int(steps[0, n]), int(steps[1, n]), int(steps[2, n]), n % 2

    def static_fin(n):
        return int(steps[0, n - 1]) if n >= 1 and steps[3, n] else None

    acc_ref[...] = jnp.zeros(acc_ref.shape, _F32)
    m_ref[...] = jnp.full(m_ref.shape, -jnp.inf, _F32)

    pipeline_step(static_step(0), None, None, None)
    pipeline_step(static_step(1), static_step(0), None, None)

    full = nsteps - 2

    combos = sorted({(int(steps[2, n + 1]), int(steps[3, n])) for n in range(full)})

    def iteration(n, par):
        k_sm = tab_ref[2, n + 1]
        k_fin = tab_ref[3, n]
        for c_sm, c_fin in combos:
            @pl.when((k_sm == c_sm) & (k_fin == c_fin))
            def _():
                pipeline_step((tab_ref[0, n + 2], tab_ref[1, n + 2], None, par),
                              (tab_ref[0, n + 1], tab_ref[1, n + 1], c_sm, 1 - par),
                              (tab_ref[0, n], tab_ref[1, n], None, par),
                              tab_ref[0, jnp.maximum(n - 1, 0)] if c_fin else None)

    def body(pair, carry):
        iteration(2 * pair, 0)
        iteration(2 * pair + 1, 1)
        return carry

    lax.fori_loop(0, full // 2, body, 0)
    if full % 2:
        iteration(full - 1, (full - 1) % 2)

    pipeline_step(None, static_step(nsteps - 1), static_step(nsteps - 2), static_fin(nsteps - 2))
    pipeline_step(None, None, static_step(nsteps - 1), static_fin(nsteps - 1))
    pipeline_step(None, None, None, int(steps[0, nsteps - 1]))


def _attention(q, k, vt, bias, lq1, lk1, lq2, lk2, subln_g, *, batch, seq, lambda_init):
    n, d_qk = q.shape
    t = ATTN_TILE
    nq = seq // t
    nh = d_qk // V_HEAD_DIM
    hb = ATTN_HEADS
    aw = vt.shape[2]
    w, ncb = ATTN_COLS, t // ATTN_COLS
    steps = _attn_steps(nq)
    assert steps.shape[1] >= 3 and nh % hb == 0
    vec = _resident((1, HEAD_DIM))
    head_rows = pl.BlockSpec((seq, hb * V_HEAD_DIM), lambda b, h: (b, h))
    return pl.pallas_call(
        functools.partial(_attn_kernel, lambda_init=lambda_init, steps=steps, heads=hb),
        grid=(batch, nh // hb),
        in_specs=[pl.BlockSpec(memory_space=pltpu.SMEM), head_rows, head_rows,
                  pl.BlockSpec((1, nq, hb * V_HEAD_DIM, t), lambda b, h: (b, 0, h, 0)),
                  pl.BlockSpec((hb, 2, t, t), lambda b, h: (h, 0, 0, 0)),
                  vec, vec, vec, vec, _resident((V_HEAD_DIM, 1))],
        out_specs=head_rows,
        out_shape=jax.ShapeDtypeStruct((n, aw), _F32),
        scratch_shapes=[pltpu.VMEM((hb, 2, ncb, t, w), _F32), pltpu.VMEM((hb, 2, ncb, t, w), _F32),
                        pltpu.VMEM((hb, 2, ncb, t, w), _BF16), pltpu.VMEM((hb, 2, ncb, t, w), _BF16),
                        pltpu.VMEM((2, hb, 2, ncb, 1, w), _F32), pltpu.VMEM((hb, 2, ncb, 1, w), _F32),
                        pltpu.VMEM((2, hb, 2, ncb, V_HEAD_DIM + _SUM_ROWS, w), _F32)],
        compiler_params=_params("parallel", "parallel"),
        name="diff_attn",
    )(jnp.asarray(steps), q, k, vt, bias, lq1, lk1, lq2, lk2, subln_g)


def _zero_after(value, zero_ref):
    bits = lax.bitcast_convert_type(value, jnp.uint32) & zero_ref[...]
    return lax.bitcast_convert_type(bits, _F32)


def _branch_mlp_kernel(x_ref, attn_ref, gate_ref, c_ref, halo_ref, wb_ref, zero_ref, lng_ref, lnb_ref,
                       wpw_ref, bpw_ref, wout_ref, g2_ref, wup_ref, wdown_ref, gf_ref, o_ref,
                       y_ref, pad_ref, sh_ref, *, final_norm, tiles_per_seq):
    s = pl.program_id(0)
    rows, d = x_ref.shape
    first_of_seq = s % tiles_per_seq == 0
    lead = CONV_HALO - (CONV_WIDTH - 1)
    span = CONV_SUB + CONV_HALO - SUBLANES
    groups = CONV_CHUNK // SUBLANES
    nchunks = rows // CONV_CHUNK
    dff = wup_ref.shape[1]
    ndots = 2 + 2 * (dff // d)

    @pl.when(s == 0)
    def _():
        y_ref[...] = jnp.zeros(y_ref.shape, _F32)

    lane_blocks = y_ref.shape[1] // CONV_LANES
    nunits = nchunks * lane_blocks

    def conv_unit(q, zero):
        k, lb = divmod(q, lane_blocks)
        lanes = slice(lb * CONV_LANES, (lb + 1) * CONV_LANES)
        sub, r0 = divmod(k * CONV_CHUNK, CONV_SUB)
        base = sub * CONV_SUB
        if r0 == 0:
            if sub == 0:
                halo = halo_ref[:, lanes]
                pad_ref[:CONV_HALO, lanes] = jnp.where(first_of_seq, jnp.zeros_like(halo), halo)
            else:
                pad_ref[:CONV_HALO, lanes] = c_ref[base - CONV_HALO:base, lanes]
            body = c_ref[base:base + CONV_SUB, lanes]
            if zero is not None:
                body = (body.reshape(CONV_SUB // SUBLANES, SUBLANES, -1) + zero[None]).reshape(body.shape)
            pad_ref[CONV_HALO:, lanes] = body
            for b in range(1, SUBLANES):
                sh_ref[b - 1, :span, lanes] = pad_ref[b:b + span, lanes]
        bias = wb_ref[CONV_WIDTH, :, lanes]
        if zero is not None:
            bias = bias + zero
        acc = [bias] * groups
        for kt in range(CONV_WIDTH):
            a, b = divmod(kt + lead, SUBLANES)
            wv = wb_ref[kt, :, lanes]
            for g in range(groups):
                r = r0 + SUBLANES * (a + g)
                xs = pad_ref[r:r + SUBLANES, lanes] if b == 0 else sh_ref[b - 1, r:r + SUBLANES, lanes]
                acc[g] = acc[g] + xs * wv
        for g in range(groups):
            r = base + r0 + SUBLANES * g
            y_ref[r:r + SUBLANES, lanes] = acc[g]

    anchors = {}
    for q in range(1, nunits):
        pos = (q - 1) * ndots / nunits
        dot_idx = int(pos)
        row = int((pos - dot_idx) * rows) // SUBLANES * SUBLANES
        anchors.setdefault(dot_idx, []).append((q, row))

    def after_dot(dot_idx, result):
        for q, row in anchors.get(dot_idx, []):
            lb = q % lane_blocks
            piece = result[row:row + SUBLANES, lb * CONV_LANES:(lb + 1) * CONV_LANES]
            conv_unit(q, _zero_after(piece, zero_ref))

    y = y_ref[...]
    mu = jnp.mean(y, axis=-1, keepdims=True)
    yc = y - mu
    var = jnp.mean(yc * yc, axis=-1, keepdims=True)
    z = yc * lax.rsqrt(var + EPS) * lng_ref[...] + lnb_ref[...]
    z = (z * jax.nn.sigmoid(z)).astype(_BF16)
    conv_unit(0, None)

    pw = _dot(z, wpw_ref[...])
    after_dot(0, pw)
    gates = gate_ref[...].astype(_F32)
    merged = gates[:, :d] * attn_ref[...] + gates[:, d:] * (pw + bpw_ref[...])
    out = _dot(merged.astype(_BF16), wout_ref[...])
    after_dot(1, out)
    h1 = x_ref[...] + out
    u = (_rms(h1) * g2_ref[...]).astype(_BF16)
    acc = jnp.zeros(h1.shape, _F32)
    for n, c0 in enumerate(range(0, dff, d)):
        up = _dot(u, wup_ref[:, c0:c0 + d])
        after_dot(2 + 2 * n, up)
        hid = jnp.maximum(up, 0.0)
        down = _dot((hid * hid).astype(_BF16), wdown_ref[c0:c0 + d, :])
        after_dot(3 + 2 * n, down)
        acc = acc + down
    h2 = h1 + acc
    if final_norm:
        h2 = _rms(h2) * gf_ref[...]
    o_ref[...] = h2


def _branch_mlp(x2, attn, gates, c, wb, ln_g, ln_b, wpw, bpw, wout, g2, wup, wdown, gf, *, seq, final_norm):
    n, d = x2.shape
    ch = c.shape[1]
    rows = MLP_ROWS
    nt = n // rows
    halo_per_tile = rows // CONV_HALO
    prev = lambda w: pl.BlockSpec((rows, w), lambda s: (jnp.maximum(s - 1, 0), 0))
    cur = lambda s: jnp.minimum(s, nt - 1)
    return pl.pallas_call(
        functools.partial(_branch_mlp_kernel, final_norm=final_norm, tiles_per_seq=seq // rows),
        grid=(nt + 1,),
        in_specs=[prev(d), prev(d), prev(2 * d),
                  pl.BlockSpec((rows, ch), lambda s: (cur(s), 0)),
                  pl.BlockSpec((CONV_HALO, ch), lambda s: (jnp.maximum(cur(s) * halo_per_tile - 1, 0), 0)),
                  _resident(wb.shape), _resident((SUBLANES, CONV_LANES)), _resident((1, ch)), _resident((1, ch)),
                  _resident(wpw.shape),
                  _resident((1, d)), _resident(wout.shape), _resident((1, d)), _resident(wup.shape),
                  _resident(wdown.shape), _resident((1, d))],
        out_specs=prev(d),
        out_shape=jax.ShapeDtypeStruct((n, d), _F32),
        scratch_shapes=[pltpu.VMEM((rows, ch), _F32),
                        pltpu.VMEM((CONV_SUB + CONV_HALO, ch), _F32),
                        pltpu.VMEM((SUBLANES - 1, CONV_SUB + CONV_HALO, ch), _F32)],
        compiler_params=_params("arbitrary"),
        name="branch_mlp",
    )(x2, attn, gates, c, c, wb, jnp.zeros((SUBLANES, CONV_LANES), jnp.uint32), ln_g, ln_b, wpw, bpw, wout, g2, wup,
      wdown, gf)


def kernel(x, rel_bias, final_norm_g, norm_mix_g, w_in, b_glu, b_gate, lam_q1, lam_k1, lam_q2, lam_k2,
           subln_g, conv_w, conv_b, conv_ln_g, conv_ln_b, w_pw2, b_pw2, w_out, norm_mlp_g, w_up, w_down):
    batch, seq, d = x.shape
    depth = w_in.shape[0]
    ch = conv_w.shape[2]
    d_qk = (w_in.shape[2] - 2 * ch - 2 * d) // 3
    c1, c2, c3 = 2 * d_qk, 3 * d_qk, 3 * d_qk + 2 * ch
    assert seq % ATTN_TILE == 0 and seq % PROJ_ROWS == 0 and PROJ_ROWS % ATTN_TILE == 0
    assert seq % MLP_ROWS == 0 and MLP_ROWS % CONV_SUB == 0 and ATTN_TILE >= MAX_DISTANCE

    row = lambda v: v.reshape(1, -1)
    h = x.reshape(batch * seq, d)
    bias = _bias_tiles(rel_bias)
    for l in range(depth):
        lambda_init = 0.8 - 0.6 * math.exp(-0.3 * l)
        w = w_in[l]
        q, k, vt, c, gates = _proj(
            h, row(norm_mix_g[l]), w[:, :c1].astype(_BF16), w[:, c1:c2].T.astype(_BF16),
            w[:, c2:c3].astype(_BF16), w[:, c3:].astype(_BF16), row(b_glu[l]), row(b_gate[l]),
            batch=batch, seq=seq)
        attn = _attention(q, k, vt, bias, row(lam_q1[l]), row(lam_k1[l]), row(lam_q2[l]), row(lam_k2[l]),
                          subln_g[l].reshape(-1, 1), batch=batch, seq=seq, lambda_init=lambda_init)
        wb = jnp.concatenate([conv_w[l], conv_b[l][None]], axis=0)
        wb = jnp.broadcast_to(wb[:, None, :], (CONV_WIDTH + 1, SUBLANES, ch))
        h = _branch_mlp(h, attn, gates, c, wb, row(conv_ln_g[l]), row(conv_ln_b[l]), w_pw2[l].astype(_BF16),
                        row(b_pw2[l]), w_out[l].astype(_BF16), row(norm_mlp_g[l]), w_up[l].astype(_BF16),
                        w_down[l].astype(_BF16), row(final_norm_g), seq=seq, final_norm=(l == depth - 1))
    return h.reshape(batch, seq, d)
```

```python
import functools
import math

import jax
import jax.numpy as jnp
import numpy as np
from jax import lax
from jax.experimental import pallas as pl
from jax.experimental.pallas import tpu as pltpu

HEAD_DIM = 64
V_HEAD_DIM = 2 * HEAD_DIM
CONV_WIDTH = 31
NUM_BUCKETS = 32
MAX_DISTANCE = 128
EPS = 1e-6
MASKED = -1e30
LOG2E = 1.4426950408889634

PROJ_ROWS = 512
ATTN_TILE = 512
ATTN_COLS = 256
ATTN_HEADS = 2
MLP_ROWS = 512
CONV_SUB = 128
CONV_HALO = 32
CONV_CHUNK = 16
CONV_LANES = 1024
SUBLANES = 8
VMEM_LIMIT = 56 * 1024 * 1024

_BF16 = jnp.bfloat16
_F32 = jnp.float32


def _resident(shape):
    return pl.BlockSpec(shape, lambda *_: (0,) * len(shape), pipeline_mode=pl.Buffered(1))


def _params(*semantics):
    return pltpu.CompilerParams(dimension_semantics=semantics, vmem_limit_bytes=VMEM_LIMIT)


def _rms(xf):
    return xf * lax.rsqrt(jnp.mean(xf * xf, axis=-1, keepdims=True) + EPS)


def _dot(a, b):
    return jnp.dot(a, b, preferred_element_type=_F32)


def _dot_nt(a, b):
    return lax.dot_general(a, b, (((1,), (1,)), ((), ())), preferred_element_type=_F32)


def _proj_kernel(x_ref, g_ref, wqk_ref, wvt_ref, wglu_ref, wgate_ref, bglu_ref, bgate_ref,
                 q_ref, k_ref, vt_ref, c_ref, gate_ref, *, d_qk, kv_tiles_per_block):
    u = (_rms(x_ref[...]) * g_ref[...]).astype(_BF16)
    ch = c_ref.shape[-1]
    for g0 in range(0, gate_ref.shape[-1], ch):
        gate_ref[:, g0:g0 + ch] = jax.nn.sigmoid(
            _dot(u, wgate_ref[:, g0:g0 + ch]) + bgate_ref[:, g0:g0 + ch]).astype(_BF16)
    ga = _dot(u, wglu_ref[:, :ch]) + bglu_ref[:, :ch]
    gb = _dot(u, wglu_ref[:, ch:]) + bglu_ref[:, ch:]
    c_ref[...] = ga * jax.nn.sigmoid(gb)
    scale = LOG2E / math.sqrt(HEAD_DIM)
    q_ref[...] = (_dot(u, wqk_ref[:, :d_qk]) * scale).astype(_BF16)
    vt = _dot_nt(wvt_ref[...], u).astype(_BF16)
    for t in range(kv_tiles_per_block):
        vt_ref[0, t] = vt[:, t * ATTN_TILE:(t + 1) * ATTN_TILE]
    k_ref[...] = _dot(u, wqk_ref[:, d_qk:]).astype(_BF16)


def _proj(x2, g, wqk, wvt, wglu, wgate, bglu, bgate, *, batch, seq):
    n, d = x2.shape
    d_qk = wqk.shape[1] // 2
    aw = wvt.shape[0]
    ch = wglu.shape[1] // 2
    ngate = wgate.shape[1]
    rows = PROJ_ROWS
    tiles = rows // ATTN_TILE
    blocks_per_seq = seq // rows
    row_spec = lambda w: pl.BlockSpec((rows, w), lambda i: (i, 0))
    return pl.pallas_call(
        functools.partial(_proj_kernel, d_qk=d_qk, kv_tiles_per_block=tiles),
        grid=(n // rows,),
        in_specs=[row_spec(d), _resident((1, d)), _resident(wqk.shape), _resident(wvt.shape),
                  _resident(wglu.shape), _resident(wgate.shape), _resident(bglu.shape),
                  _resident(bgate.shape)],
        out_specs=[row_spec(d_qk), row_spec(d_qk),
                   pl.BlockSpec((1, tiles, aw, ATTN_TILE),
                                lambda i: (i // blocks_per_seq, i % blocks_per_seq, 0, 0)),
                   row_spec(ch), row_spec(ngate)],
        out_shape=[jax.ShapeDtypeStruct((n, d_qk), _BF16), jax.ShapeDtypeStruct((n, d_qk), _BF16),
                   jax.ShapeDtypeStruct((batch, seq // ATTN_TILE, aw, ATTN_TILE), _BF16),
                   jax.ShapeDtypeStruct((n, ch), _F32), jax.ShapeDtypeStruct((n, ngate), _BF16)],
        compiler_params=_params("parallel"),
        name="proj",
    )(x2, g, wqk, wvt, wglu, wgate, bglu, bgate)


def _bucket_tiles():
    t = ATTN_TILE
    r = np.arange(t, dtype=np.int64)[:, None]
    c = np.arange(t, dtype=np.int64)[None, :]

    def bucket(rel):
        n = np.maximum(rel, 0)
        max_exact = NUM_BUCKETS // 2
        nf = np.maximum(n, 1).astype(np.float32)
        large = max_exact + (np.log(nf / np.float32(max_exact)) / np.float32(math.log(MAX_DISTANCE / max_exact))
                             * np.float32(NUM_BUCKETS - max_exact)).astype(np.int32)
        large = np.minimum(large, NUM_BUCKETS - 1)
        return np.where(n < max_exact, n, large).astype(np.int32)

    diag = np.where(c - r >= 0, bucket(c - r), -1)
    left = bucket(t + c - r)
    return np.stack([diag, left]).astype(np.int32)


def _bias_kernel(rb_ref, bucket_ref, out_ref):
    h = pl.program_id(0)
    bucket = bucket_ref[0]
    far = rb_ref[NUM_BUCKETS - 1, h]
    acc = jnp.full(bucket.shape, MASKED, _F32)
    for b in range(NUM_BUCKETS):
        acc = jnp.where(bucket == b, (rb_ref[b, h] - far) * LOG2E, acc)
    out_ref[0, 0] = acc


def _bias_tiles(rel_bias):
    nh = rel_bias.shape[1]
    t = ATTN_TILE
    buckets = jnp.asarray(_bucket_tiles())
    return pl.pallas_call(
        _bias_kernel,
        grid=(nh, 2),
        in_specs=[pl.BlockSpec(memory_space=pltpu.SMEM),
                  pl.BlockSpec((1, t, t), lambda h, j: (j, 0, 0))],
        out_specs=pl.BlockSpec((1, 1, t, t), lambda h, j: (h, j, 0, 0)),
        out_shape=jax.ShapeDtypeStruct((nh, 2, t, t), _F32),
        compiler_params=_params("parallel", "parallel"),
        name="bias_tiles",
    )(rel_bias, buckets)


MAX_CHAINS = 8
_PLAIN, _DIAG = 0, 1
_SUM_ROWS = 16


def _attn_steps(nq):
    steps = [[i, j, _DIAG if j == i else _PLAIN, 0] for i in range(nq) for j in range(i + 1)]
    for n in range(1, len(steps)):
        steps[n][3] = int(steps[n - 1][2] == _DIAG)
    return np.asarray(steps, dtype=np.int32).T.copy()


def _col_max(s):
    groups = [s[r:r + SUBLANES] for r in range(0, s.shape[0], SUBLANES)]
    acc = groups[:MAX_CHAINS]
    for n, g in enumerate(groups[MAX_CHAINS:]):
        acc[n % MAX_CHAINS] = jnp.maximum(acc[n % MAX_CHAINS], g)
    while len(acc) > 1:
        acc = [jnp.maximum(acc[n], acc[n + 1]) for n in range(0, len(acc), 2)]
    return jnp.max(acc[0], axis=0, keepdims=True)


def _tile_rows(idx, offset=0, size=ATTN_TILE):
    start = idx * ATTN_TILE + offset
    if not isinstance(start, int):
        start = pl.multiple_of(start, math.gcd(ATTN_TILE, offset))
    return pl.ds(start, size)


def _attn_kernel(tab_ref, q_ref, k_ref, vt_ref, bias_ref, lq1_ref, lk1_ref, lq2_ref, lk2_ref, g_ref,
                 o_ref, s0_ref, s1_ref, p0_ref, p1_ref, alpha_ref, m_ref, acc_ref, *, lambda_init, steps, heads):
    t, w = ATTN_TILE, ATTN_COLS
    s_ref = (s0_ref, s1_ref)
    p_ref = (p0_ref, p1_ref)
    nsteps = steps.shape[1]
    parts = [(hh, a, c) for hh in range(heads) for a in range(2) for c in range(0, t, w)]

    def head_cols(hh):
        return slice(hh * V_HEAD_DIM, (hh + 1) * V_HEAD_DIM)

    def scores(i, j, slot, hh, a, c):
        q = q_ref[_tile_rows(i, c, w), head_cols(hh)]
        kk = k_ref[_tile_rows(j), head_cols(hh)]
        lane = lax.broadcasted_iota(jnp.int32, q.shape, 1)
        keep = lane < HEAD_DIM if a == 0 else lane >= HEAD_DIM
        s_ref[slot][hh, a, c // w] = _dot_nt(kk, jnp.where(keep, q, jnp.zeros_like(q)))

    def softmax(i, j, slot, kind, hh, a, c):
        cb = c // w
        if kind == _PLAIN:
            if c == 0:
                near = jnp.where(j == i - 1, 1.0, 0.0).astype(_F32)
                s_ref[slot][hh, a, cb, t - MAX_DISTANCE:, :MAX_DISTANCE] += (
                    bias_ref[hh, 1, t - MAX_DISTANCE:, :MAX_DISTANCE] * near)
            s = s_ref[slot][hh, a, cb]
        else:
            s = s_ref[slot][hh, a, cb] + bias_ref[hh, 0, :, c:c + w]
        m_old = jnp.where(j == 0, -jnp.inf, m_ref[hh, a, cb])
        m_new = jnp.maximum(m_old, _col_max(s))
        alpha_ref[slot, hh, a, cb] = jnp.exp2(m_old - m_new)
        p_ref[slot][hh, a, cb] = jnp.exp2(s - m_new).astype(_BF16)
        m_ref[hh, a, cb] = m_new

    def values(i, j, slot, hh, a, c):
        cb = c // w
        vt = jnp.concatenate([vt_ref[0, j, head_cols(hh), :], jnp.ones((_SUM_ROWS, t), _BF16)], axis=0)
        acc_ref[i % 2, hh, a, cb] = (alpha_ref[slot, hh, a, cb] * acc_ref[i % 2, hh, a, cb]
                                     + _dot(vt, p_ref[slot][hh, a, cb]))

    def finalize(i, hh, cb):
        lam = (jnp.exp(jnp.sum(lq1_ref[...] * lk1_ref[...])) - jnp.exp(jnp.sum(lq2_ref[...] * lk2_ref[...]))
               + lambda_init)
        a1 = acc_ref[i % 2, hh, 0, cb]
        a2 = acc_ref[i % 2, hh, 1, cb]
        r1 = 1.0 / a1[V_HEAD_DIM:V_HEAD_DIM + 1]
        r2 = lam / a2[V_HEAD_DIM:V_HEAD_DIM + 1]
        o = a1[:V_HEAD_DIM] * r1 - a2[:V_HEAD_DIM] * r2
        o = o * lax.rsqrt(jnp.mean(o * o, axis=0, keepdims=True) + EPS)
        o = o * (g_ref[...] * (1.0 - lambda_init))
        o_ref[_tile_rows(i, cb * w, w), head_cols(hh)] = o.T

    fin_pieces = [(hh, cb) for hh in range(heads) for cb in range(t // w)]

    def pipeline_step(sc, sm, pv, fin):
        pending = list(fin_pieces) if fin is not None else []
        stride = max(len(parts) // max(len(fin_pieces), 1), 1)
        for n, (hh, a, c) in enumerate(parts):
            if sc is not None:
                scores(sc[0], sc[1], sc[3], hh, a, c)
            if pv is not None:
                values(pv[0], pv[1], pv[3], hh, a, c)
            if sm is not None:
                softmax(sm[0], sm[1], sm[3], sm[2], hh, a, c)
            if pending and n % stride == stride - 1:
                finalize(fin, *pending.pop(0))
        while pending:
            finalize(fin, *pending.pop(0))

    def static_step(n):
        return int(steps[0, n]), int(steps[1, n]), int(steps[2, n]), n % 2

    def static_fin(n):
        return int(steps[0, n - 1]) if n >= 1 and steps[3, n] else None

    acc_ref[...] = jnp.zeros(acc_ref.shape, _F32)
    m_ref[...] = jnp.full(m_ref.shape, -jnp.inf, _F32)

    pipeline_step(static_step(0), None, None, None)
    pipeline_step(static_step(1), static_step(0), None, None)

    full = nsteps - 2

    combos = sorted({(int(steps[2, n + 1]), int(steps[3, n])) for n in range(full)})

    def iteration(n, par):
        k_sm = tab_ref[2, n + 1]
        k_fin = tab_ref[3, n]
        for c_sm, c_fin in combos:
            @pl.when((k_sm == c_sm) & (k_fin == c_fin))
            def _():
                pipeline_step((tab_ref[0, n + 2], tab_ref[1, n + 2], None, par),
                              (tab_ref[0, n + 1], tab_ref[1, n + 1], c_sm, 1 - par),
                              (tab_ref[0, n], tab_ref[1, n], None, par),
                              tab_ref[0, jnp.maximum(n - 1, 0)] if c_fin else None)

    def body(pair, carry):
        iteration(2 * pair, 0)
        iteration(2 * pair + 1, 1)
        return carry

    lax.fori_loop(0, full // 2, body, 0)
    if full % 2:
        iteration(full - 1, (full - 1) % 2)

    pipeline_step(None, static_step(nsteps - 1), static_step(nsteps - 2), static_fin(nsteps - 2))
    pipeline_step(None, None, static_step(nsteps - 1), static_fin(nsteps - 1))
    pipeline_step(None, None, None, int(steps[0, nsteps - 1]))


def _attention(q, k, vt, bias, lq1, lk1, lq2, lk2, subln_g, *, batch, seq, lambda_init):
    n, d_qk = q.shape
    t = ATTN_TILE
    nq = seq // t
    nh = d_qk // V_HEAD_DIM
    hb = ATTN_HEADS
    aw = vt.shape[2]
    w, ncb = ATTN_COLS, t // ATTN_COLS
    steps = _attn_steps(nq)
    assert steps.shape[1] >= 3 and nh % hb == 0
    vec = _resident((1, HEAD_DIM))
    head_rows = pl.BlockSpec((seq, hb * V_HEAD_DIM), lambda b, h: (b, h))
    return pl.pallas_call(
        functools.partial(_attn_kernel, lambda_init=lambda_init, steps=steps, heads=hb),
        grid=(batch, nh // hb),
        in_specs=[pl.BlockSpec(memory_space=pltpu.SMEM), head_rows, head_rows,
                  pl.BlockSpec((1, nq, hb * V_HEAD_DIM, t), lambda b, h: (b, 0, h, 0)),
                  pl.BlockSpec((hb, 2, t, t), lambda b, h: (h, 0, 0, 0)),
                  vec, vec, vec, vec, _resident((V_HEAD_DIM, 1))],
        out_specs=head_rows,
        out_shape=jax.ShapeDtypeStruct((n, aw), _F32),
        scratch_shapes=[pltpu.VMEM((hb, 2, ncb, t, w), _F32), pltpu.VMEM((hb, 2, ncb, t, w), _F32),
                        pltpu.VMEM((hb, 2, ncb, t, w), _BF16), pltpu.VMEM((hb, 2, ncb, t, w), _BF16),
                        pltpu.VMEM((2, hb, 2, ncb, 1, w), _F32), pltpu.VMEM((hb, 2, ncb, 1, w), _F32),
                        pltpu.VMEM((2, hb, 2, ncb, V_HEAD_DIM + _SUM_ROWS, w), _F32)],
        compiler_params=_params("parallel", "parallel"),
        name="diff_attn",
    )(jnp.asarray(steps), q, k, vt, bias, lq1, lk1, lq2, lk2, subln_g)


def _zero_after(value, zero_ref):
    bits = lax.bitcast_convert_type(value, jnp.uint32) & zero_ref[...]
    return lax.bitcast_convert_type(bits, _F32)


def _branch_mlp_kernel(x_ref, attn_ref, gate_ref, c_ref, halo_ref, wb_ref, zero_ref, lng_ref, lnb_ref,
                       wpw_ref, bpw_ref, wout_ref, g2_ref, wup_ref, wdown_ref, gf_ref, o_ref,
                       y_ref, pad_ref, sh_ref, *, final_norm, tiles_per_seq):
    s = pl.program_id(0)
    rows, d = x_ref.shape
    first_of_seq = s % tiles_per_seq == 0
    lead = CONV_HALO - (CONV_WIDTH - 1)
    span = CONV_SUB + CONV_HALO - SUBLANES
    groups = CONV_CHUNK // SUBLANES
    nchunks = rows // CONV_CHUNK
    dff = wup_ref.shape[1]
    ndots = 2 + 2 * (dff // d)

    @pl.when(s == 0)
    def _():
        y_ref[...] = jnp.zeros(y_ref.shape, _F32)

    lane_blocks = y_ref.shape[1] // CONV_LANES
    nunits = nchunks * lane_blocks

    def conv_unit(q, zero):
        k, lb = divmod(q, lane_blocks)
        lanes = slice(lb * CONV_LANES, (lb + 1) * CONV_LANES)
        sub, r0 = divmod(k * CONV_CHUNK, CONV_SUB)
        base = sub * CONV_SUB
        if r0 == 0:
            if sub == 0:
                halo = halo_ref[:, lanes]
                pad_ref[:CONV_HALO, lanes] = jnp.where(first_of_seq, jnp.zeros_like(halo), halo)
            else:
                pad_ref[:CONV_HALO, lanes] = c_ref[base - CONV_HALO:base, lanes]
            body = c_ref[base:base + CONV_SUB, lanes]
            if zero is not None:
                body = (body.reshape(CONV_SUB // SUBLANES, SUBLANES, -1) + zero[None]).reshape(body.shape)
            pad_ref[CONV_HALO:, lanes] = body
            for b in range(1, SUBLANES):
                sh_ref[b - 1, :span, lanes] = pad_ref[b:b + span, lanes]
        bias = wb_ref[CONV_WIDTH, :, lanes]
        if zero is not None:
            bias = bias + zero
        acc = [bias] * groups
        for kt in range(CONV_WIDTH):
            a, b = divmod(kt + lead, SUBLANES)
            wv = wb_ref[kt, :, lanes]
            for g in range(groups):
                r = r0 + SUBLANES * (a + g)
                xs = pad_ref[r:r + SUBLANES, lanes] if b == 0 else sh_ref[b - 1, r:r + SUBLANES, lanes]
                acc[g] = acc[g] + xs * wv
        for g in range(groups):
            r = base + r0 + SUBLANES * g
            y_ref[r:r + SUBLANES, lanes] = acc[g]

    anchors = {}
    for q in range(1, nunits):
        pos = (q - 1) * ndots / nunits
        dot_idx = int(pos)
        row = int((pos - dot_idx) * rows) // SUBLANES * SUBLANES
        anchors.setdefault(dot_idx, []).append((q, row))

    def after_dot(dot_idx, result):
        for q, row in anchors.get(dot_idx, []):
            lb = q % lane_blocks
            piece = result[row:row + SUBLANES, lb * CONV_LANES:(lb + 1) * CONV_LANES]
            conv_unit(q, _zero_after(piece, zero_ref))

    y = y_ref[...]
    mu = jnp.mean(y, axis=-1, keepdims=True)
    yc = y - mu
    var = jnp.mean(yc * yc, axis=-1, keepdims=True)
    z = yc * lax.rsqrt(var + EPS) * lng_ref[...] + lnb_ref[...]
    z = (z * jax.nn.sigmoid(z)).astype(_BF16)
    conv_unit(0, None)

    pw = _dot(z, wpw_ref[...])
    after_dot(0, pw)
    gates = gate_ref[...].astype(_F32)
    merged = gates[:, :d] * attn_ref[...] + gates[:, d:] * (pw + bpw_ref[...])
    out = _dot(merged.astype(_BF16), wout_ref[...])
    after_dot(1, out)
    h1 = x_ref[...] + out
    u = (_rms(h1) * g2_ref[...]).astype(_BF16)
    acc = jnp.zeros(h1.shape, _F32)
    for n, c0 in enumerate(range(0, dff, d)):
        up = _dot(u, wup_ref[:, c0:c0 + d])
        after_dot(2 + 2 * n, up)
        hid = jnp.maximum(up, 0.0)
        down = _dot((hid * hid).astype(_BF16), wdown_ref[c0:c0 + d, :])
        after_dot(3 + 2 * n, down)
        acc = acc + down
    h2 = h1 + acc
    if final_norm:
        h2 = _rms(h2) * gf_ref[...]
    o_ref[...] = h2


def _branch_mlp(x2, attn, gates, c, wb, ln_g, ln_b, wpw, bpw, wout, g2, wup, wdown, gf, *, seq, final_norm):
    n, d = x2.shape
    ch = c.shape[1]
    rows = MLP_ROWS
    nt = n // rows
    halo_per_tile = rows // CONV_HALO
    prev = lambda w: pl.BlockSpec((rows, w), lambda s: (jnp.maximum(s - 1, 0), 0))
    cur = lambda s: jnp.minimum(s, nt - 1)
    return pl.pallas_call(
        functools.partial(_branch_mlp_kernel, final_norm=final_norm, tiles_per_seq=seq // rows),
        grid=(nt + 1,),
        in_specs=[prev(d), prev(d), prev(2 * d),
                  pl.BlockSpec((rows, ch), lambda s: (cur(s), 0)),
                  pl.BlockSpec((CONV_HALO, ch), lambda s: (jnp.maximum(cur(s) * halo_per_tile - 1, 0), 0)),
                  _resident(wb.shape), _resident((SUBLANES, CONV_LANES)), _resident((1, ch)), _resident((1, ch)),
                  _resident(wpw.shape),
                  _resident((1, d)), _resident(wout.shape), _resident((1, d)), _resident(wup.shape),
                  _resident(wdown.shape), _resident((1, d))],
        out_specs=prev(d),
        out_shape=jax.ShapeDtypeStruct((n, d), _F32),
        scratch_shapes=[pltpu.VMEM((rows, ch), _F32),
                        pltpu.VMEM((CONV_SUB + CONV_HALO, ch), _F32),
                        pltpu.VMEM((SUBLANES - 1, CONV_SUB + CONV_HALO, ch), _F32)],
        compiler_params=_params("arbitrary"),
        name="branch_mlp",
    )(x2, attn, gates, c, c, wb, jnp.zeros((SUBLANES, CONV_LANES), jnp.uint32), ln_g, ln_b, wpw, bpw, wout, g2, wup,
      wdown, gf)


def kernel(x, rel_bias, final_norm_g, norm_mix_g, w_in, b_glu, b_gate, lam_q1, lam_k1, lam_q2, lam_k2,
           subln_g, conv_w, conv_b, conv_ln_g, conv_ln_b, w_pw2, b_pw2, w_out, norm_mlp_g, w_up, w_down):
    batch, seq, d = x.shape
    depth = w_in.shape[0]
    ch = conv_w.shape[2]
    d_qk = (w_in.shape[2] - 2 * ch - 2 * d) // 3
    c1, c2, c3 = 2 * d_qk, 3 * d_qk, 3 * d_qk + 2 * ch
    assert seq % ATTN_TILE == 0 and seq % PROJ_ROWS == 0 and PROJ_ROWS % ATTN_TILE == 0
    assert seq % MLP_ROWS == 0 and MLP_ROWS % CONV_SUB == 0 and ATTN_TILE >= MAX_DISTANCE

    row = lambda v: v.reshape(1, -1)
    h = x.reshape(batch * seq, d)
    bias = _bias_tiles(rel_bias)
    for l in range(depth):
        lambda_init = 0.8 - 0.6 * math.exp(-0.3 * l)
        w = w_in[l]
        q, k, vt, c, gates = _proj(
            h, row(norm_mix_g[l]), w[:, :c1].astype(_BF16), w[:, c1:c2].T.astype(_BF16),
            w[:, c2:c3].astype(_BF16), w[:, c3:].astype(_BF16), row(b_glu[l]), row(b_gate[l]),
            batch=batch, seq=seq)
        attn = _attention(q, k, vt, bias, row(lam_q1[l]), row(lam_k1[l]), row(lam_q2[l]), row(lam_k2[l]),
                          subln_g[l].reshape(-1, 1), batch=batch, seq=seq, lambda_init=lambda_init)
        wb = jnp.concatenate([conv_w[l], conv_b[l][None]], axis=0)
        wb = jnp.broadcast_to(wb[:, None, :], (CONV_WIDTH + 1, SUBLANES, ch))
        h = _branch_mlp(h, attn, gates, c, wb, row(conv_ln_g[l]), row(conv_ln_b[l]), w_pw2[l].astype(_BF16),
                        row(b_pw2[l]), w_out[l].astype(_BF16), row(norm_mlp_g[l]), w_up[l].astype(_BF16),
                        w_down[l].astype(_BF16), row(final_norm_g), seq=seq, final_norm=(l == depth - 1))
    return h.reshape(batch, seq, d)
```

```python
import functools
import math

import jax
import jax.numpy as jnp
import numpy as np
from jax import lax
from jax.experimental import pallas as pl
from jax.experimental.pallas import tpu as pltpu

HEAD_DIM = 64
V_HEAD_DIM = 2 * HEAD_DIM
CONV_WIDTH = 31
NUM_BUCKETS = 32
MAX_DISTANCE = 128
EPS = 1e-6
MASKED = -1e30
LOG2E = 1.4426950408889634

PROJ_ROWS = 512
ATTN_TILE = 512
ATTN_COLS = 256
ATTN_HEADS = 2
MLP_ROWS = 512
CONV_SUB = 128
CONV_HALO = 32
CONV_CHUNK = 16
CONV_LANES = 1024
SUBLANES = 8
VMEM_LIMIT = 56 * 1024 * 1024

_BF16 = jnp.bfloat16
_F32 = jnp.float32


def _resident(shape):
    return pl.BlockSpec(shape, lambda *_: (0,) * len(shape), pipeline_mode=pl.Buffered(1))


def _params(*semantics):
    return pltpu.CompilerParams(dimension_semantics=semantics, vmem_limit_bytes=VMEM_LIMIT)


def _rms(xf):
    return xf * lax.rsqrt(jnp.mean(xf * xf, axis=-1, keepdims=True) + EPS)


def _dot(a, b):
    return jnp.dot(a, b, preferred_element_type=_F32)


def _dot_nt(a, b):
    return lax.dot_general(a, b, (((1,), (1,)), ((), ())), preferred_element_type=_F32)


def _proj_kernel(x_ref, g_ref, wqk_ref, wvt_ref, wglu_ref, wgate_ref, bglu_ref, bgate_ref,
                 q_ref, k_ref, vt_ref, c_ref, gate_ref, *, d_qk, kv_tiles_per_block):
    u = (_rms(x_ref[...]) * g_ref[...]).astype(_BF16)
    ch = c_ref.shape[-1]
    for g0 in range(0, gate_ref.shape[-1], ch):
        gate_ref[:, g0:g0 + ch] = jax.nn.sigmoid(
            _dot(u, wgate_ref[:, g0:g0 + ch]) + bgate_ref[:, g0:g0 + ch]).astype(_BF16)
    ga = _dot(u, wglu_ref[:, :ch]) + bglu_ref[:, :ch]
    gb = _dot(u, wglu_ref[:, ch:]) + bglu_ref[:, ch:]
    c_ref[...] = ga * jax.nn.sigmoid(gb)
    scale = LOG2E / math.sqrt(HEAD_DIM)
    q_ref[...] = (_dot(u, wqk_ref[:, :d_qk]) * scale).astype(_BF16)
    vt = _dot_nt(wvt_ref[...], u).astype(_BF16)
    for t in range(kv_tiles_per_block):
        vt_ref[0, t] = vt[:, t * ATTN_TILE:(t + 1) * ATTN_TILE]
    k_ref[...] = _dot(u, wqk_ref[:, d_qk:]).astype(_BF16)


def _proj(x2, g, wqk, wvt, wglu, wgate, bglu, bgate, *, batch, seq):
    n, d = x2.shape
    d_qk = wqk.shape[1] // 2
    aw = wvt.shape[0]
    ch = wglu.shape[1] // 2
    ngate = wgate.shape[1]
    rows = PROJ_ROWS
    tiles = rows // ATTN_TILE
    blocks_per_seq = seq // rows
    row_spec = lambda w: pl.BlockSpec((rows, w), lambda i: (i, 0))
    return pl.pallas_call(
        functools.partial(_proj_kernel, d_qk=d_qk, kv_tiles_per_block=tiles),
        grid=(n // rows,),
        in_specs=[row_spec(d), _resident((1, d)), _resident(wqk.shape), _resident(wvt.shape),
                  _resident(wglu.shape), _resident(wgate.shape), _resident(bglu.shape),
                  _resident(bgate.shape)],
        out_specs=[row_spec(d_qk), row_spec(d_qk),
                   pl.BlockSpec((1, tiles, aw, ATTN_TILE),
                                lambda i: (i // blocks_per_seq, i % blocks_per_seq, 0, 0)),
                   row_spec(ch), row_spec(ngate)],
        out_shape=[jax.ShapeDtypeStruct((n, d_qk), _BF16), jax.ShapeDtypeStruct((n, d_qk), _BF16),
                   jax.ShapeDtypeStruct((batch, seq // ATTN_TILE, aw, ATTN_TILE), _BF16),
                   jax.ShapeDtypeStruct((n, ch), _F32), jax.ShapeDtypeStruct((n, ngate), _BF16)],
        compiler_params=_params("parallel"),
        name="proj",
    )(x2, g, wqk, wvt, wglu, wgate, bglu, bgate)


def _bucket_table():
    t, w = ATTN_TILE, ATTN_COLS
    r = np.arange(t, dtype=np.int64)[:, None]
    c = np.arange(w, dtype=np.int64)[None, :]
    rel = (t - w) + c - r
    n = np.maximum(rel, 0)
    max_exact = NUM_BUCKETS // 2
    nf = np.maximum(n, 1).astype(np.float32)
    large = max_exact + (np.log(nf / np.float32(max_exact)) / np.float32(math.log(MAX_DISTANCE / max_exact))
                         * np.float32(NUM_BUCKETS - max_exact)).astype(np.int32)
    large = np.minimum(large, NUM_BUCKETS - 1)
    return np.where(rel >= 0, np.where(n < max_exact, n, large), -1).astype(np.int32)


def _bias_kernel(rb_ref, bucket_ref, out_ref):
    h = pl.program_id(0)
    bucket = bucket_ref[...]
    far = rb_ref[NUM_BUCKETS - 1, h]
    acc = jnp.full(bucket.shape, MASKED, _F32)
    for b in range(NUM_BUCKETS):
        acc = jnp.where(bucket == b, (rb_ref[b, h] - far) * LOG2E, acc)
    out_ref[0] = acc


def _bias_tiles(rel_bias):
    nh = rel_bias.shape[1]
    t, w = ATTN_TILE, ATTN_COLS
    return pl.pallas_call(
        _bias_kernel,
        grid=(nh,),
        in_specs=[pl.BlockSpec(memory_space=pltpu.SMEM), _resident((t, w))],
        out_specs=pl.BlockSpec((1, t, w), lambda h: (h, 0, 0)),
        out_shape=jax.ShapeDtypeStruct((nh, t, w), _F32),
        compiler_params=_params("parallel"),
        name="bias_tiles",
    )(rel_bias, jnp.asarray(_bucket_table()))


MAX_CHAINS = 8
_PLAIN, _DIAG = 0, 1
_SUM_ROWS = 16


def _attn_steps(nq):
    steps = [[i, j, _DIAG if j == i else _PLAIN, 0] for i in range(nq) for j in range(i + 1)]
    for n in range(1, len(steps)):
        steps[n][3] = int(steps[n - 1][2] == _DIAG)
    return np.asarray(steps, dtype=np.int32).T.copy()


def _col_max(s):
    groups = [s[r:r + SUBLANES] for r in range(0, s.shape[0], SUBLANES)]
    acc = groups[:MAX_CHAINS]
    for n, g in enumerate(groups[MAX_CHAINS:]):
        acc[n % MAX_CHAINS] = jnp.maximum(acc[n % MAX_CHAINS], g)
    while len(acc) > 1:
        acc = [jnp.maximum(acc[n], acc[n + 1]) for n in range(0, len(acc), 2)]
    return jnp.max(acc[0], axis=0, keepdims=True)


def _tile_rows(idx, offset=0, size=ATTN_TILE):
    start = idx * ATTN_TILE + offset
    if not isinstance(start, int):
        start = pl.multiple_of(start, math.gcd(ATTN_TILE, offset))
    return pl.ds(start, size)


def _attn_kernel(tab_ref, q_ref, k_ref, vt_ref, bias_ref, lq1_ref, lk1_ref, lq2_ref, lk2_ref, g_ref,
                 o_ref, s0_ref, s1_ref, p0_ref, p1_ref, alpha_ref, m_ref, acc_ref, *, lambda_init, steps, heads):
    t, w = ATTN_TILE, ATTN_COLS
    s_ref = (s0_ref, s1_ref)
    p_ref = (p0_ref, p1_ref)
    nsteps = steps.shape[1]
    parts = [(hh, a, c) for hh in range(heads) for a in range(2) for c in range(0, t, w)]

    def head_cols(hh):
        return slice(hh * V_HEAD_DIM, (hh + 1) * V_HEAD_DIM)

    def keys_used(kind, c):
        return c + w if kind == _DIAG else t

    def scores(i, j, slot, kind, hh, a, c):
        nk = keys_used(kind, c)
        q = q_ref[_tile_rows(i, c, w), head_cols(hh)]
        kk = k_ref[_tile_rows(j, 0, nk), head_cols(hh)]
        lane = lax.broadcasted_iota(jnp.int32, q.shape, 1)
        keep = lane < HEAD_DIM if a == 0 else lane >= HEAD_DIM
        s_ref[slot][hh, a, c // w, :nk] = _dot_nt(kk, jnp.where(keep, q, jnp.zeros_like(q)))

    def softmax(i, j, slot, kind, hh, a, c):
        cb = c // w
        nk = keys_used(kind, c)
        if kind == _PLAIN:
            if c == 0:
                near = jnp.where(j == i - 1, 1.0, 0.0).astype(_F32)
                s_ref[slot][hh, a, cb, t - MAX_DISTANCE:, :MAX_DISTANCE] += (
                    bias_ref[hh, t - w - MAX_DISTANCE:t - w, :MAX_DISTANCE] * near)
            s = s_ref[slot][hh, a, cb]
        else:
            s = s_ref[slot][hh, a, cb, :nk] + bias_ref[hh, t - nk:, :]
        m_old = jnp.where(j == 0, -jnp.inf, m_ref[hh, a, cb])
        m_new = jnp.maximum(m_old, _col_max(s))
        alpha_ref[slot, hh, a, cb] = jnp.exp2(m_old - m_new)
        p_ref[slot][hh, a, cb, :nk] = jnp.exp2(s - m_new).astype(_BF16)
        m_ref[hh, a, cb] = m_new

    def values(i, j, slot, kind, hh, a, c):
        cb = c // w
        nk = keys_used(kind, c)
        vt = jnp.concatenate([vt_ref[0, j, head_cols(hh), :nk], jnp.ones((_SUM_ROWS, nk), _BF16)], axis=0)
        acc_ref[i % 2, hh, a, cb] = (alpha_ref[slot, hh, a, cb] * acc_ref[i % 2, hh, a, cb]
                                     + _dot(vt, p_ref[slot][hh, a, cb, :nk]))

    def finalize(i, hh, cb):
        lam = (jnp.exp(jnp.sum(lq1_ref[...] * lk1_ref[...])) - jnp.exp(jnp.sum(lq2_ref[...] * lk2_ref[...]))
               + lambda_init)
        a1 = acc_ref[i % 2, hh, 0, cb]
        a2 = acc_ref[i % 2, hh, 1, cb]
        r1 = 1.0 / a1[V_HEAD_DIM:V_HEAD_DIM + 1]
        r2 = lam / a2[V_HEAD_DIM:V_HEAD_DIM + 1]
        o = a1[:V_HEAD_DIM] * r1 - a2[:V_HEAD_DIM] * r2
        o = o * lax.rsqrt(jnp.mean(o * o, axis=0, keepdims=True) + EPS)
        o = o * (g_ref[...] * (1.0 - lambda_init))
        o_ref[_tile_rows(i, cb * w, w), head_cols(hh)] = o.T

    fin_pieces = [(hh, cb) for hh in range(heads) for cb in range(t // w)]

    def pipeline_step(sc, sm, pv, fin):
        pending = list(fin_pieces) if fin is not None else []
        stride = max(len(parts) // max(len(fin_pieces), 1), 1)
        for n, (hh, a, c) in enumerate(parts):
            if sc is not None:
                scores(sc[0], sc[1], sc[3], sc[2], hh, a, c)
            if pv is not None:
                values(pv[0], pv[1], pv[3], pv[2], hh, a, c)
            if sm is not None:
                softmax(sm[0], sm[1], sm[3], sm[2], hh, a, c)
            if pending and n % stride == stride - 1:
                finalize(fin, *pending.pop(0))
        while pending:
            finalize(fin, *pending.pop(0))

    def static_step(n):
        return int(steps[0, n]), int(steps[1, n]), int(steps[2, n]), n % 2

    def static_fin(n):
        return int(steps[0, n - 1]) if n >= 1 and steps[3, n] else None

    acc_ref[...] = jnp.zeros(acc_ref.shape, _F32)
    m_ref[...] = jnp.full(m_ref.shape, -jnp.inf, _F32)

    pipeline_step(static_step(0), None, None, None)
    pipeline_step(static_step(1), static_step(0), None, None)

    full = nsteps - 2
    combos = sorted({tuple(int(v) for v in (steps[2, n + 2], steps[2, n + 1], steps[2, n], steps[3, n]))
                     for n in range(full)})

    def iteration(n, par):
        kinds = (tab_ref[2, n + 2], tab_ref[2, n + 1], tab_ref[2, n], tab_ref[3, n])
        for combo in combos:
            c_sc, c_sm, c_pv, c_fin = combo

            @pl.when(functools.reduce(lambda x, y: x & y, [k == c for k, c in zip(kinds, combo)]))
            def _():
                pipeline_step((tab_ref[0, n + 2], tab_ref[1, n + 2], c_sc, par),
                              (tab_ref[0, n + 1], tab_ref[1, n + 1], c_sm, 1 - par),
                              (tab_ref[0, n], tab_ref[1, n], c_pv, par),
                              tab_ref[0, jnp.maximum(n - 1, 0)] if c_fin else None)

    def body(pair, carry):
        iteration(2 * pair, 0)
        iteration(2 * pair + 1, 1)
        return carry

    lax.fori_loop(0, full // 2, body, 0)
    if full % 2:
        iteration(full - 1, (full - 1) % 2)

    pipeline_step(None, static_step(nsteps - 1), static_step(nsteps - 2), static_fin(nsteps - 2))
    pipeline_step(None, None, static_step(nsteps - 1), static_fin(nsteps - 1))
    pipeline_step(None, None, None, int(steps[0, nsteps - 1]))


def _attention(q, k, vt, bias, lq1, lk1, lq2, lk2, subln_g, *, batch, seq, lambda_init):
    n, d_qk = q.shape
    t = ATTN_TILE
    nq = seq // t
    nh = d_qk // V_HEAD_DIM
    hb = ATTN_HEADS
    aw = vt.shape[2]
    w, ncb = ATTN_COLS, t // ATTN_COLS
    steps = _attn_steps(nq)
    assert steps.shape[1] >= 3 and nh % hb == 0
    vec = _resident((1, HEAD_DIM))
    head_rows = pl.BlockSpec((seq, hb * V_HEAD_DIM), lambda b, h: (b, h))
    return pl.pallas_call(
        functools.partial(_attn_kernel, lambda_init=lambda_init, steps=steps, heads=hb),
        grid=(batch, nh // hb),
        in_specs=[pl.BlockSpec(memory_space=pltpu.SMEM), head_rows, head_rows,
                  pl.BlockSpec((1, nq, hb * V_HEAD_DIM, t), lambda b, h: (b, 0, h, 0)),
                  pl.BlockSpec((hb, t, w), lambda b, h: (h, 0, 0)),
                  vec, vec, vec, vec, _resident((V_HEAD_DIM, 1))],
        out_specs=head_rows,
        out_shape=jax.ShapeDtypeStruct((n, aw), _F32),
        scratch_shapes=[pltpu.VMEM((hb, 2, ncb, t, w), _F32), pltpu.VMEM((hb, 2, ncb, t, w), _F32),
                        pltpu.VMEM((hb, 2, ncb, t, w), _BF16), pltpu.VMEM((hb, 2, ncb, t, w), _BF16),
                        pltpu.VMEM((2, hb, 2, ncb, 1, w), _F32), pltpu.VMEM((hb, 2, ncb, 1, w), _F32),
                        pltpu.VMEM((2, hb, 2, ncb, V_HEAD_DIM + _SUM_ROWS, w), _F32)],
        compiler_params=_params("parallel", "parallel"),
        name="diff_attn",
    )(jnp.asarray(steps), q, k, vt, bias, lq1, lk1, lq2, lk2, subln_g)


def _zero_after(value, zero_ref):
    bits = lax.bitcast_convert_type(value, jnp.uint32) & zero_ref[...]
    return lax.bitcast_convert_type(bits, _F32)


def _branch_mlp_kernel(x_ref, attn_ref, gate_ref, c_ref, halo_ref, wb_ref, zero_ref, lng_ref, lnb_ref,
                       wpw_ref, bpw_ref, wout_ref, g2_ref, wup_ref, wdown_ref, gf_ref, o_ref,
                       y_ref, pad_ref, sh_ref, *, final_norm, tiles_per_seq):
    s = pl.program_id(0)
    rows, d = x_ref.shape
    first_of_seq = s % tiles_per_seq == 0
    lead = CONV_HALO - (CONV_WIDTH - 1)
    span = CONV_SUB + CONV_HALO - SUBLANES
    groups = CONV_CHUNK // SUBLANES
    nchunks = rows // CONV_CHUNK
    dff = wup_ref.shape[1]
    ndots = 2 + 2 * (dff // d)

    @pl.when(s == 0)
    def _():
        y_ref[...] = jnp.zeros(y_ref.shape, _F32)

    lane_blocks = y_ref.shape[1] // CONV_LANES
    nunits = nchunks * lane_blocks

    def conv_unit(q, zero):
        k, lb = divmod(q, lane_blocks)
        lanes = slice(lb * CONV_LANES, (lb + 1) * CONV_LANES)
        sub, r0 = divmod(k * CONV_CHUNK, CONV_SUB)
        base = sub * CONV_SUB
        if r0 == 0:
            if sub == 0:
                halo = halo_ref[:, lanes]
                pad_ref[:CONV_HALO, lanes] = jnp.where(first_of_seq, jnp.zeros_like(halo), halo)
            else:
                pad_ref[:CONV_HALO, lanes] = c_ref[base - CONV_HALO:base, lanes]
            body = c_ref[base:base + CONV_SUB, lanes]
            if zero is not None:
                body = (body.reshape(CONV_SUB // SUBLANES, SUBLANES, -1) + zero[None]).reshape(body.shape)
            pad_ref[CONV_HALO:, lanes] = body
            for b in range(1, SUBLANES):
                sh_ref[b - 1, :span, lanes] = pad_ref[b:b + span, lanes]
        bias = wb_ref[CONV_WIDTH, :, lanes]
        if zero is not None:
            bias = bias + zero
        acc = [bias] * groups
        for kt in range(CONV_WIDTH):
            a, b = divmod(kt + lead, SUBLANES)
            wv = wb_ref[kt, :, lanes]
            for g in range(groups):
                r = r0 + SUBLANES * (a + g)
                xs = pad_ref[r:r + SUBLANES, lanes] if b == 0 else sh_ref[b - 1, r:r + SUBLANES, lanes]
                acc[g] = acc[g] + xs * wv
        for g in range(groups):
            r = base + r0 + SUBLANES * g
            y_ref[r:r + SUBLANES, lanes] = acc[g]

    anchors = {}
    for q in range(1, nunits):
        pos = (q - 1) * ndots / nunits
        dot_idx = int(pos)
        row = int((pos - dot_idx) * rows) // SUBLANES * SUBLANES
        anchors.setdefault(dot_idx, []).append((q, row))

    def after_dot(dot_idx, result):
        for q, row in anchors.get(dot_idx, []):
            lb = q % lane_blocks
            piece = result[row:row + SUBLANES, lb * CONV_LANES:(lb + 1) * CONV_LANES]
            conv_unit(q, _zero_after(piece, zero_ref))

    y = y_ref[...]
    mu = jnp.mean(y, axis=-1, keepdims=True)
    yc = y - mu
    var = jnp.mean(yc * yc, axis=-1, keepdims=True)
    z = yc * lax.rsqrt(var + EPS) * lng_ref[...] + lnb_ref[...]
    z = (z * jax.nn.sigmoid(z)).astype(_BF16)
    conv_unit(0, None)

    pw = _dot(z, wpw_ref[...])
    after_dot(0, pw)
    gates = gate_ref[...].astype(_F32)
    merged = gates[:, :d] * attn_ref[...] + gates[:, d:] * (pw + bpw_ref[...])
    out = _dot(merged.astype(_BF16), wout_ref[...])
    after_dot(1, out)
    h1 = x_ref[...] + out
    u = (_rms(h1) * g2_ref[...]).astype(_BF16)
    acc = jnp.zeros(h1.shape, _F32)
    for n, c0 in enumerate(range(0, dff, d)):
        up = _dot(u, wup_ref[:, c0:c0 + d])
        after_dot(2 + 2 * n, up)
        hid = jnp.maximum(up, 0.0)
        down = _dot((hid * hid).astype(_BF16), wdown_ref[c0:c0 + d, :])
        after_dot(3 + 2 * n, down)
        acc = acc + down
    h2 = h1 + acc
    if final_norm:
        h2 = _rms(h2) * gf_ref[...]
    o_ref[...] = h2


def _branch_mlp(x2, attn, gates, c, wb, ln_g, ln_b, wpw, bpw, wout, g2, wup, wdown, gf, *, seq, final_norm):
    n, d = x2.shape
    ch = c.shape[1]
    rows = MLP_ROWS
    nt = n // rows
    halo_per_tile = rows // CONV_HALO
    prev = lambda w: pl.BlockSpec((rows, w), lambda s: (jnp.maximum(s - 1, 0), 0))
    cur = lambda s: jnp.minimum(s, nt - 1)
    return pl.pallas_call(
        functools.partial(_branch_mlp_kernel, final_norm=final_norm, tiles_per_seq=seq // rows),
        grid=(nt + 1,),
        in_specs=[prev(d), prev(d), prev(2 * d),
                  pl.BlockSpec((rows, ch), lambda s: (cur(s), 0)),
                  pl.BlockSpec((CONV_HALO, ch), lambda s: (jnp.maximum(cur(s) * halo_per_tile - 1, 0), 0)),
                  _resident(wb.shape), _resident((SUBLANES, CONV_LANES)), _resident((1, ch)), _resident((1, ch)),
                  _resident(wpw.shape),
                  _resident((1, d)), _resident(wout.shape), _resident((1, d)), _resident(wup.shape),
                  _resident(wdown.shape), _resident((1, d))],
        out_specs=prev(d),
        out_shape=jax.ShapeDtypeStruct((n, d), _F32),
        scratch_shapes=[pltpu.VMEM((rows, ch), _F32),
                        pltpu.VMEM((CONV_SUB + CONV_HALO, ch), _F32),
                        pltpu.VMEM((SUBLANES - 1, CONV_SUB + CONV_HALO, ch), _F32)],
        compiler_params=_params("arbitrary"),
        name="branch_mlp",
    )(x2, attn, gates, c, c, wb, jnp.zeros((SUBLANES, CONV_LANES), jnp.uint32), ln_g, ln_b, wpw, bpw, wout, g2, wup,
      wdown, gf)


def kernel(x, rel_bias, final_norm_g, norm_mix_g, w_in, b_glu, b_gate, lam_q1, lam_k1, lam_q2, lam_k2,
           subln_g, conv_w, conv_b, conv_ln_g, conv_ln_b, w_pw2, b_pw2, w_out, norm_mlp_g, w_up, w_down):
    batch, seq, d = x.shape
    depth = w_in.shape[0]
    ch = conv_w.shape[2]
    d_qk = (w_in.shape[2] - 2 * ch - 2 * d) // 3
    c1, c2, c3 = 2 * d_qk, 3 * d_qk, 3 * d_qk + 2 * ch
    assert seq % ATTN_TILE == 0 and seq % PROJ_ROWS == 0 and PROJ_ROWS % ATTN_TILE == 0
    assert seq % MLP_ROWS == 0 and MLP_ROWS % CONV_SUB == 0 and ATTN_TILE >= MAX_DISTANCE

    row = lambda v: v.reshape(1, -1)
    h = x.reshape(batch * seq, d)
    bias = _bias_tiles(rel_bias)
    for l in range(depth):
        lambda_init = 0.8 - 0.6 * math.exp(-0.3 * l)
        w = w_in[l]
        q, k, vt, c, gates = _proj(
            h, row(norm_mix_g[l]), w[:, :c1].astype(_BF16), w[:, c1:c2].T.astype(_BF16),
            w[:, c2:c3].astype(_BF16), w[:, c3:].astype(_BF16), row(b_glu[l]), row(b_gate[l]),
            batch=batch, seq=seq)
        attn = _attention(q, k, vt, bias, row(lam_q1[l]), row(lam_k1[l]), row(lam_q2[l]), row(lam_k2[l]),
                          subln_g[l].reshape(-1, 1), batch=batch, seq=seq, lambda_init=lambda_init)
        wb = jnp.concatenate([conv_w[l], conv_b[l][None]], axis=0)
        wb = jnp.broadcast_to(wb[:, None, :], (CONV_WIDTH + 1, SUBLANES, ch))
        h = _branch_mlp(h, attn, gates, c, wb, row(conv_ln_g[l]), row(conv_ln_b[l]), w_pw2[l].astype(_BF16),
                        row(b_pw2[l]), w_out[l].astype(_BF16), row(norm_mlp_g[l]), w_up[l].astype(_BF16),
                        w_down[l].astype(_BF16), row(final_norm_g), seq=seq, final_norm=(l == depth - 1))
    return h.reshape(batch, seq, d)
```

```python
import functools
import math

import jax
import jax.numpy as jnp
import numpy as np
from jax import lax
from jax.experimental import pallas as pl
from jax.experimental.pallas import tpu as pltpu

HEAD_DIM = 64
V_HEAD_DIM = 2 * HEAD_DIM
CONV_WIDTH = 31
NUM_BUCKETS = 32
MAX_DISTANCE = 128
EPS = 1e-6
MASKED = -1e30
LOG2E = 1.4426950408889634

PROJ_ROWS = 512
ATTN_TILE = 512
ATTN_COLS = 256
ATTN_HEADS = 2
MLP_ROWS = 512
CONV_SUB = 128
CONV_HALO = 32
CONV_CHUNK = 16
CONV_LANES = 1024
SUBLANES = 8
VMEM_LIMIT = 56 * 1024 * 1024

_BF16 = jnp.bfloat16
_F32 = jnp.float32


def _resident(shape):
    return pl.BlockSpec(shape, lambda *_: (0,) * len(shape), pipeline_mode=pl.Buffered(1))


def _params(*semantics):
    return pltpu.CompilerParams(dimension_semantics=semantics, vmem_limit_bytes=VMEM_LIMIT)


def _rms(xf):
    return xf * lax.rsqrt(jnp.mean(xf * xf, axis=-1, keepdims=True) + EPS)


def _dot(a, b):
    return jnp.dot(a, b, preferred_element_type=_F32)


def _dot_nt(a, b):
    return lax.dot_general(a, b, (((1,), (1,)), ((), ())), preferred_element_type=_F32)


def _proj_kernel(x_ref, g_ref, wqk_ref, wvt_ref, wglu_ref, wgate_ref, bglu_ref, bgate_ref,
                 q_ref, k_ref, vt_ref, c_ref, gate_ref, *, d_qk, kv_tiles_per_block):
    u = (_rms(x_ref[...]) * g_ref[...]).astype(_BF16)
    ch = c_ref.shape[-1]
    for g0 in range(0, gate_ref.shape[-1], ch):
        gate_ref[:, g0:g0 + ch] = jax.nn.sigmoid(
            _dot(u, wgate_ref[:, g0:g0 + ch]) + bgate_ref[:, g0:g0 + ch]).astype(_BF16)
    ga = _dot(u, wglu_ref[:, :ch]) + bglu_ref[:, :ch]
    gb = _dot(u, wglu_ref[:, ch:]) + bglu_ref[:, ch:]
    c_ref[...] = ga * jax.nn.sigmoid(gb)
    scale = LOG2E / math.sqrt(HEAD_DIM)
    q_ref[...] = (_dot(u, wqk_ref[:, :d_qk]) * scale).astype(_BF16)
    vt = _dot_nt(wvt_ref[...], u).astype(_BF16)
    for t in range(kv_tiles_per_block):
        vt_ref[0, t] = vt[:, t * ATTN_TILE:(t + 1) * ATTN_TILE]
    k_ref[...] = _dot(u, wqk_ref[:, d_qk:]).astype(_BF16)


def _proj(x2, g, wqk, wvt, wglu, wgate, bglu, bgate, *, batch, seq):
    n, d = x2.shape
    d_qk = wqk.shape[1] // 2
    aw = wvt.shape[0]
    ch = wglu.shape[1] // 2
    ngate = wgate.shape[1]
    rows = PROJ_ROWS
    tiles = rows // ATTN_TILE
    blocks_per_seq = seq // rows
    row_spec = lambda w: pl.BlockSpec((rows, w), lambda i: (i, 0))
    return pl.pallas_call(
        functools.partial(_proj_kernel, d_qk=d_qk, kv_tiles_per_block=tiles),
        grid=(n // rows,),
        in_specs=[row_spec(d), _resident((1, d)), _resident(wqk.shape), _resident(wvt.shape),
                  _resident(wglu.shape), _resident(wgate.shape), _resident(bglu.shape),
                  _resident(bgate.shape)],
        out_specs=[row_spec(d_qk), row_spec(d_qk),
                   pl.BlockSpec((1, tiles, aw, ATTN_TILE),
                                lambda i: (i // blocks_per_seq, i % blocks_per_seq, 0, 0)),
                   row_spec(ch), row_spec(ngate)],
        out_shape=[jax.ShapeDtypeStruct((n, d_qk), _BF16), jax.ShapeDtypeStruct((n, d_qk), _BF16),
                   jax.ShapeDtypeStruct((batch, seq // ATTN_TILE, aw, ATTN_TILE), _BF16),
                   jax.ShapeDtypeStruct((n, ch), _F32), jax.ShapeDtypeStruct((n, ngate), _BF16)],
        compiler_params=_params("parallel"),
        name="proj",
    )(x2, g, wqk, wvt, wglu, wgate, bglu, bgate)


def _bucket_table():
    t, w = ATTN_TILE, ATTN_COLS
    r = np.arange(t, dtype=np.int64)[:, None]
    c = np.arange(w, dtype=np.int64)[None, :]
    rel = (t - w) + c - r
    n = np.maximum(rel, 0)
    max_exact = NUM_BUCKETS // 2
    nf = np.maximum(n, 1).astype(np.float32)
    large = max_exact + (np.log(nf / np.float32(max_exact)) / np.float32(math.log(MAX_DISTANCE / max_exact))
                         * np.float32(NUM_BUCKETS - max_exact)).astype(np.int32)
    large = np.minimum(large, NUM_BUCKETS - 1)
    return np.where(rel >= 0, np.where(n < max_exact, n, large), -1).astype(np.int32)


def _bias_kernel(rb_ref, bucket_ref, out_ref):
    h = pl.program_id(0)
    bucket = bucket_ref[...]
    far = rb_ref[NUM_BUCKETS - 1, h]
    acc = jnp.full(bucket.shape, MASKED, _F32)
    for b in range(NUM_BUCKETS):
        acc = jnp.where(bucket == b, (rb_ref[b, h] - far) * LOG2E, acc)
    out_ref[0] = acc


def _bias_tiles(rel_bias):
    nh = rel_bias.shape[1]
    t, w = ATTN_TILE, ATTN_COLS
    return pl.pallas_call(
        _bias_kernel,
        grid=(nh,),
        in_specs=[pl.BlockSpec(memory_space=pltpu.SMEM), _resident((t, w))],
        out_specs=pl.BlockSpec((1, t, w), lambda h: (h, 0, 0)),
        out_shape=jax.ShapeDtypeStruct((nh, t, w), _F32),
        compiler_params=_params("parallel"),
        name="bias_tiles",
    )(rel_bias, jnp.asarray(_bucket_table()))


MAX_CHAINS = 8
_PLAIN, _DIAG = 0, 1
_SUM_ROWS = 16


def _attn_steps(nq):
    steps = [[i, j, _DIAG if j == i else _PLAIN, 0] for i in range(nq) for j in range(i + 1)]
    for n in range(1, len(steps)):
        steps[n][3] = int(steps[n - 1][2] == _DIAG)
    return np.asarray(steps, dtype=np.int32).T.copy()


def _col_max(s):
    groups = [s[r:r + SUBLANES] for r in range(0, s.shape[0], SUBLANES)]
    acc = groups[:MAX_CHAINS]
    for n, g in enumerate(groups[MAX_CHAINS:]):
        acc[n % MAX_CHAINS] = jnp.maximum(acc[n % MAX_CHAINS], g)
    while len(acc) > 1:
        acc = [jnp.maximum(acc[n], acc[n + 1]) for n in range(0, len(acc), 2)]
    return jnp.max(acc[0], axis=0, keepdims=True)


def _tile_rows(idx, offset=0, size=ATTN_TILE):
    start = idx * ATTN_TILE + offset
    if not isinstance(start, int):
        start = pl.multiple_of(start, math.gcd(ATTN_TILE, offset))
    return pl.ds(start, size)


def _attn_kernel(tab_ref, q_ref, k_ref, vt_ref, bias_ref, lq1_ref, lk1_ref, lq2_ref, lk2_ref, g_ref,
                 o_ref, s0_ref, s1_ref, p0_ref, p1_ref, alpha_ref, m_ref, acc_ref, *, lambda_init, steps, heads):
    t, w = ATTN_TILE, ATTN_COLS
    s_ref = (s0_ref, s1_ref)
    p_ref = (p0_ref, p1_ref)
    nsteps = steps.shape[1]
    parts = [(hh, a, c) for hh in range(heads) for a in range(2) for c in range(0, t, w)]

    def head_cols(hh):
        return slice(hh * V_HEAD_DIM, (hh + 1) * V_HEAD_DIM)

    def scores(i, j, slot, hh, a, c):
        q = q_ref[_tile_rows(i, c, w), head_cols(hh)]
        kk = k_ref[_tile_rows(j), head_cols(hh)]
        lane = lax.broadcasted_iota(jnp.int32, q.shape, 1)
        keep = lane < HEAD_DIM if a == 0 else lane >= HEAD_DIM
        s_ref[slot][hh, a, c // w] = _dot_nt(kk, jnp.where(keep, q, jnp.zeros_like(q)))

    def softmax(i, j, slot, kind, hh, a, c):
        cb = c // w
        if kind == _PLAIN:
            if c == 0:
                near = jnp.where(j == i - 1, 1.0, 0.0).astype(_F32)
                s_ref[slot][hh, a, cb, t - MAX_DISTANCE:, :MAX_DISTANCE] += (
                    bias_ref[hh, t - w - MAX_DISTANCE:t - w, :MAX_DISTANCE] * near)
            s = s_ref[slot][hh, a, cb]
        else:
            bias = bias_ref[hh, t - w - c:, :]
            if c + w < t:
                bias = jnp.concatenate([bias, jnp.full((t - w - c, w), MASKED, _F32)], axis=0)
            s = s_ref[slot][hh, a, cb] + bias
        m_old = jnp.where(j == 0, -jnp.inf, m_ref[hh, a, cb])
        m_new = jnp.maximum(m_old, _col_max(s))
        alpha_ref[slot, hh, a, cb] = jnp.exp2(m_old - m_new)
        p_ref[slot][hh, a, cb] = jnp.exp2(s - m_new).astype(_BF16)
        m_ref[hh, a, cb] = m_new

    def values(i, j, slot, hh, a, c):
        cb = c // w
        vt = jnp.concatenate([vt_ref[0, j, head_cols(hh), :], jnp.ones((_SUM_ROWS, t), _BF16)], axis=0)
        acc_ref[i % 2, hh, a, cb] = (alpha_ref[slot, hh, a, cb] * acc_ref[i % 2, hh, a, cb]
                                     + _dot(vt, p_ref[slot][hh, a, cb]))

    def finalize(i, hh, cb):
        lam = (jnp.exp(jnp.sum(lq1_ref[...] * lk1_ref[...])) - jnp.exp(jnp.sum(lq2_ref[...] * lk2_ref[...]))
               + lambda_init)
        a1 = acc_ref[i % 2, hh, 0, cb]
        a2 = acc_ref[i % 2, hh, 1, cb]
        r1 = 1.0 / a1[V_HEAD_DIM:V_HEAD_DIM + 1]
        r2 = lam / a2[V_HEAD_DIM:V_HEAD_DIM + 1]
        o = a1[:V_HEAD_DIM] * r1 - a2[:V_HEAD_DIM] * r2
        o = o * lax.rsqrt(jnp.mean(o * o, axis=0, keepdims=True) + EPS)
        o = o * (g_ref[...] * (1.0 - lambda_init))
        o_ref[_tile_rows(i, cb * w, w), head_cols(hh)] = o.T

    fin_pieces = [(hh, cb) for hh in range(heads) for cb in range(t // w)]

    def pipeline_step(sc, sm, pv, fin):
        pending = list(fin_pieces) if fin is not None else []
        stride = max(len(parts) // max(len(fin_pieces), 1), 1)
        for n, (hh, a, c) in enumerate(parts):
            if sc is not None:
                scores(sc[0], sc[1], sc[3], hh, a, c)
            if pv is not None:
                values(pv[0], pv[1], pv[3], hh, a, c)
            if sm is not None:
                softmax(sm[0], sm[1], sm[3], sm[2], hh, a, c)
            if pending and n % stride == stride - 1:
                finalize(fin, *pending.pop(0))
        while pending:
            finalize(fin, *pending.pop(0))

    def static_step(n):
        return int(steps[0, n]), int(steps[1, n]), int(steps[2, n]), n % 2

    def static_fin(n):
        return int(steps[0, n - 1]) if n >= 1 and steps[3, n] else None

    acc_ref[...] = jnp.zeros(acc_ref.shape, _F32)
    m_ref[...] = jnp.full(m_ref.shape, -jnp.inf, _F32)

    pipeline_step(static_step(0), None, None, None)
    pipeline_step(static_step(1), static_step(0), None, None)

    full = nsteps - 2
    combos = sorted({(int(steps[2, n + 1]), int(steps[3, n])) for n in range(full)})

    def iteration(n, par):
        k_sm = tab_ref[2, n + 1]
        k_fin = tab_ref[3, n]
        for c_sm, c_fin in combos:
            @pl.when((k_sm == c_sm) & (k_fin == c_fin))
            def _():
                pipeline_step((tab_ref[0, n + 2], tab_ref[1, n + 2], None, par),
                              (tab_ref[0, n + 1], tab_ref[1, n + 1], c_sm, 1 - par),
                              (tab_ref[0, n], tab_ref[1, n], None, par),
                              tab_ref[0, jnp.maximum(n - 1, 0)] if c_fin else None)

    def body(pair, carry):
        iteration(2 * pair, 0)
        iteration(2 * pair + 1, 1)
        return carry

    lax.fori_loop(0, full // 2, body, 0)
    if full % 2:
        iteration(full - 1, (full - 1) % 2)

    pipeline_step(None, static_step(nsteps - 1), static_step(nsteps - 2), static_fin(nsteps - 2))
    pipeline_step(None, None, static_step(nsteps - 1), static_fin(nsteps - 1))
    pipeline_step(None, None, None, int(steps[0, nsteps - 1]))


def _attention(q, k, vt, bias, lq1, lk1, lq2, lk2, subln_g, *, batch, seq, lambda_init):
    n, d_qk = q.shape
    t = ATTN_TILE
    nq = seq // t
    nh = d_qk // V_HEAD_DIM
    hb = ATTN_HEADS
    aw = vt.shape[2]
    w, ncb = ATTN_COLS, t // ATTN_COLS
    steps = _attn_steps(nq)
    assert steps.shape[1] >= 3 and nh % hb == 0
    vec = _resident((1, HEAD_DIM))
    head_rows = pl.BlockSpec((seq, hb * V_HEAD_DIM), lambda b, h: (b, h))
    return pl.pallas_call(
        functools.partial(_attn_kernel, lambda_init=lambda_init, steps=steps, heads=hb),
        grid=(batch, nh // hb),
        in_specs=[pl.BlockSpec(memory_space=pltpu.SMEM), head_rows, head_rows,
                  pl.BlockSpec((1, nq, hb * V_HEAD_DIM, t), lambda b, h: (b, 0, h, 0)),
                  pl.BlockSpec((hb, t, w), lambda b, h: (h, 0, 0)),
                  vec, vec, vec, vec, _resident((V_HEAD_DIM, 1))],
        out_specs=head_rows,
        out_shape=jax.ShapeDtypeStruct((n, aw), _F32),
        scratch_shapes=[pltpu.VMEM((hb, 2, ncb, t, w), _F32), pltpu.VMEM((hb, 2, ncb, t, w), _F32),
                        pltpu.VMEM((hb, 2, ncb, t, w), _BF16), pltpu.VMEM((hb, 2, ncb, t, w), _BF16),
                        pltpu.VMEM((2, hb, 2, ncb, 1, w), _F32), pltpu.VMEM((hb, 2, ncb, 1, w), _F32),
                        pltpu.VMEM((2, hb, 2, ncb, V_HEAD_DIM + _SUM_ROWS, w), _F32)],
        compiler_params=_params("parallel", "parallel"),
        name="diff_attn",
    )(jnp.asarray(steps), q, k, vt, bias, lq1, lk1, lq2, lk2, subln_g)


def _zero_after(value, zero_ref):
    bits = lax.bitcast_convert_type(value, jnp.uint32) & zero_ref[...]
    return lax.bitcast_convert_type(bits, _F32)


def _branch_mlp_kernel(x_ref, attn_ref, gate_ref, c_ref, halo_ref, wb_ref, zero_ref, lng_ref, lnb_ref,
                       wpw_ref, bpw_ref, wout_ref, g2_ref, wup_ref, wdown_ref, gf_ref, o_ref,
                       y_ref, pad_ref, sh_ref, *, final_norm, tiles_per_seq):
    s = pl.program_id(0)
    rows, d = x_ref.shape
    first_of_seq = s % tiles_per_seq == 0
    lead = CONV_HALO - (CONV_WIDTH - 1)
    span = CONV_SUB + CONV_HALO - SUBLANES
    groups = CONV_CHUNK // SUBLANES
    nchunks = rows // CONV_CHUNK
    dff = wup_ref.shape[1]
    ndots = 2 + 2 * (dff // d)

    @pl.when(s == 0)
    def _():
        y_ref[...] = jnp.zeros(y_ref.shape, _F32)

    lane_blocks = y_ref.shape[1] // CONV_LANES
    nunits = nchunks * lane_blocks

    def conv_unit(q, zero):
        k, lb = divmod(q, lane_blocks)
        lanes = slice(lb * CONV_LANES, (lb + 1) * CONV_LANES)
        sub, r0 = divmod(k * CONV_CHUNK, CONV_SUB)
        base = sub * CONV_SUB
        if r0 == 0:
            if sub == 0:
                halo = halo_ref[:, lanes]
                pad_ref[:CONV_HALO, lanes] = jnp.where(first_of_seq, jnp.zeros_like(halo), halo)
            else:
                pad_ref[:CONV_HALO, lanes] = c_ref[base - CONV_HALO:base, lanes]
            body = c_ref[base:base + CONV_SUB, lanes]
            if zero is not None:
                body = (body.reshape(CONV_SUB // SUBLANES, SUBLANES, -1) + zero[None]).reshape(body.shape)
            pad_ref[CONV_HALO:, lanes] = body
            for b in range(1, SUBLANES):
                sh_ref[b - 1, :span, lanes] = pad_ref[b:b + span, lanes]
        bias = wb_ref[CONV_WIDTH, :, lanes]
        if zero is not None:
            bias = bias + zero
        acc = [bias] * groups
        for kt in range(CONV_WIDTH):
            a, b = divmod(kt + lead, SUBLANES)
            wv = wb_ref[kt, :, lanes]
            for g in range(groups):
                r = r0 + SUBLANES * (a + g)
                xs = pad_ref[r:r + SUBLANES, lanes] if b == 0 else sh_ref[b - 1, r:r + SUBLANES, lanes]
                acc[g] = acc[g] + xs * wv
        for g in range(groups):
            r = base + r0 + SUBLANES * g
            y_ref[r:r + SUBLANES, lanes] = acc[g]

    anchors = {}
    for q in range(1, nunits):
        pos = (q - 1) * ndots / nunits
        dot_idx = int(pos)
        row = int((pos - dot_idx) * rows) // SUBLANES * SUBLANES
        anchors.setdefault(dot_idx, []).append((q, row))

    def after_dot(dot_idx, result):
        for q, row in anchors.get(dot_idx, []):
            lb = q % lane_blocks
            piece = result[row:row + SUBLANES, lb * CONV_LANES:(lb + 1) * CONV_LANES]
            conv_unit(q, _zero_after(piece, zero_ref))

    y = y_ref[...]
    mu = jnp.mean(y, axis=-1, keepdims=True)
    yc = y - mu
    var = jnp.mean(yc * yc, axis=-1, keepdims=True)
    z = yc * lax.rsqrt(var + EPS) * lng_ref[...] + lnb_ref[...]
    z = (z * jax.nn.sigmoid(z)).astype(_BF16)
    conv_unit(0, None)

    pw = _dot(z, wpw_ref[...])
    after_dot(0, pw)
    gates = gate_ref[...].astype(_F32)
    merged = gates[:, :d] * attn_ref[...] + gates[:, d:] * (pw + bpw_ref[...])
    out = _dot(merged.astype(_BF16), wout_ref[...])
    after_dot(1, out)
    h1 = x_ref[...] + out
    u = (_rms(h1) * g2_ref[...]).astype(_BF16)
    acc = jnp.zeros(h1.shape, _F32)
    for n, c0 in enumerate(range(0, dff, d)):
        up = _dot(u, wup_ref[:, c0:c0 + d])
        after_dot(2 + 2 * n, up)
        hid = jnp.maximum(up, 0.0)
        down = _dot((hid * hid).astype(_BF16), wdown_ref[c0:c0 + d, :])
        after_dot(3 + 2 * n, down)
        acc = acc + down
    h2 = h1 + acc
    if final_norm:
        h2 = _rms(h2) * gf_ref[...]
    o_ref[...] = h2


def _branch_mlp(x2, attn, gates, c, wb, ln_g, ln_b, wpw, bpw, wout, g2, wup, wdown, gf, *, seq, final_norm):
    n, d = x2.shape
    ch = c.shape[1]
    rows = MLP_ROWS
    nt = n // rows
    halo_per_tile = rows // CONV_HALO
    prev = lambda w: pl.BlockSpec((rows, w), lambda s: (jnp.maximum(s - 1, 0), 0))
    cur = lambda s: jnp.minimum(s, nt - 1)
    return pl.pallas_call(
        functools.partial(_branch_mlp_kernel, final_norm=final_norm, tiles_per_seq=seq // rows),
        grid=(nt + 1,),
        in_specs=[prev(d), prev(d), prev(2 * d),
                  pl.BlockSpec((rows, ch), lambda s: (cur(s), 0)),
                  pl.BlockSpec((CONV_HALO, ch), lambda s: (jnp.maximum(cur(s) * halo_per_tile - 1, 0), 0)),
                  _resident(wb.shape), _resident((SUBLANES, CONV_LANES)), _resident((1, ch)), _resident((1, ch)),
                  _resident(wpw.shape),
                  _resident((1, d)), _resident(wout.shape), _resident((1, d)), _resident(wup.shape),
                  _resident(wdown.shape), _resident((1, d))],
        out_specs=prev(d),
        out_shape=jax.ShapeDtypeStruct((n, d), _F32),
        scratch_shapes=[pltpu.VMEM((rows, ch), _F32),
                        pltpu.VMEM((CONV_SUB + CONV_HALO, ch), _F32),
                        pltpu.VMEM((SUBLANES - 1, CONV_SUB + CONV_HALO, ch), _F32)],
        compiler_params=_params("arbitrary"),
        name="branch_mlp",
    )(x2, attn, gates, c, c, wb, jnp.zeros((SUBLANES, CONV_LANES), jnp.uint32), ln_g, ln_b, wpw, bpw, wout, g2, wup,
      wdown, gf)


def kernel(x, rel_bias, final_norm_g, norm_mix_g, w_in, b_glu, b_gate, lam_q1, lam_k1, lam_q2, lam_k2,
           subln_g, conv_w, conv_b, conv_ln_g, conv_ln_b, w_pw2, b_pw2, w_out, norm_mlp_g, w_up, w_down):
    batch, seq, d = x.shape
    depth = w_in.shape[0]
    ch = conv_w.shape[2]
    d_qk = (w_in.shape[2] - 2 * ch - 2 * d) // 3
    c1, c2, c3 = 2 * d_qk, 3 * d_qk, 3 * d_qk + 2 * ch
    assert seq % ATTN_TILE == 0 and seq % PROJ_ROWS == 0 and PROJ_ROWS % ATTN_TILE == 0
    assert seq % MLP_ROWS == 0 and MLP_ROWS % CONV_SUB == 0 and ATTN_TILE >= MAX_DISTANCE

    row = lambda v: v.reshape(1, -1)
    h = x.reshape(batch * seq, d)
    bias = _bias_tiles(rel_bias)
    for l in range(depth):
        lambda_init = 0.8 - 0.6 * math.exp(-0.3 * l)
        w = w_in[l]
        q, k, vt, c, gates = _proj(
            h, row(norm_mix_g[l]), w[:, :c1].astype(_BF16), w[:, c1:c2].T.astype(_BF16),
            w[:, c2:c3].astype(_BF16), w[:, c3:].astype(_BF16), row(b_glu[l]), row(b_gate[l]),
            batch=batch, seq=seq)
        attn = _attention(q, k, vt, bias, row(lam_q1[l]), row(lam_k1[l]), row(lam_q2[l]), row(lam_k2[l]),
                          subln_g[l].reshape(-1, 1), batch=batch, seq=seq, lambda_init=lambda_init)
        wb = jnp.concatenate([conv_w[l], conv_b[l][None]], axis=0)
        wb = jnp.broadcast_to(wb[:, None, :], (CONV_WIDTH + 1, SUBLANES, ch))
        h = _branch_mlp(h, attn, gates, c, wb, row(conv_ln_g[l]), row(conv_ln_b[l]), w_pw2[l].astype(_BF16),
                        row(b_pw2[l]), w_out[l].astype(_BF16), row(norm_mlp_g[l]), w_up[l].astype(_BF16),
                        w_down[l].astype(_BF16), row(final_norm_g), seq=seq, final_norm=(l == depth - 1))
    return h.reshape(batch, seq, d)
```

```python
import functools
import math

import jax
import jax.numpy as jnp
import numpy as np
from jax import lax
from jax.experimental import pallas as pl
from jax.experimental.pallas import tpu as pltpu

HEAD_DIM = 64
V_HEAD_DIM = 2 * HEAD_DIM
CONV_WIDTH = 31
NUM_BUCKETS = 32
MAX_DISTANCE = 128
EPS = 1e-6
MASKED = -1e30
LOG2E = 1.4426950408889634

PROJ_ROWS = 512
ATTN_TILE = 512
ATTN_COLS = 256
ATTN_HEADS = 2
MLP_ROWS = 512
CONV_SUB = 128
CONV_HALO = 32
CONV_CHUNK = 16
CONV_LANES = 1024
SUBLANES = 8
VMEM_LIMIT = 56 * 1024 * 1024

_BF16 = jnp.bfloat16
_F32 = jnp.float32


def _resident(shape):
    return pl.BlockSpec(shape, lambda *_: (0,) * len(shape), pipeline_mode=pl.Buffered(1))


def _params(*semantics):
    return pltpu.CompilerParams(dimension_semantics=semantics, vmem_limit_bytes=VMEM_LIMIT)


def _rms(xf):
    return xf * lax.rsqrt(jnp.mean(xf * xf, axis=-1, keepdims=True) + EPS)


def _dot(a, b):
    return jnp.dot(a, b, preferred_element_type=_F32)


def _dot_nt(a, b):
    return lax.dot_general(a, b, (((1,), (1,)), ((), ())), preferred_element_type=_F32)


def _proj_kernel(x_ref, g_ref, wqk_ref, wvt_ref, wglu_ref, wgate_ref, bglu_ref, bgate_ref,
                 q_ref, k_ref, vt_ref, c_ref, gate_ref, *, d_qk, kv_tiles_per_block):
    u = (_rms(x_ref[...]) * g_ref[...]).astype(_BF16)
    ch = c_ref.shape[-1]
    for g0 in range(0, gate_ref.shape[-1], ch):
        gate_ref[:, g0:g0 + ch] = jax.nn.sigmoid(
            _dot(u, wgate_ref[:, g0:g0 + ch]) + bgate_ref[:, g0:g0 + ch]).astype(_BF16)
    ga = _dot(u, wglu_ref[:, :ch]) + bglu_ref[:, :ch]
    gb = _dot(u, wglu_ref[:, ch:]) + bglu_ref[:, ch:]
    c_ref[...] = ga * jax.nn.sigmoid(gb)
    scale = LOG2E / math.sqrt(HEAD_DIM)
    q_ref[...] = (_dot(u, wqk_ref[:, :d_qk]) * scale).astype(_BF16)
    vt = _dot_nt(wvt_ref[...], u).astype(_BF16)
    for t in range(kv_tiles_per_block):
        vt_ref[0, t] = vt[:, t * ATTN_TILE:(t + 1) * ATTN_TILE]
    k_ref[...] = _dot(u, wqk_ref[:, d_qk:]).astype(_BF16)


def _proj(x2, g, wqk, wvt, wglu, wgate, bglu, bgate, *, batch, seq):
    n, d = x2.shape
    d_qk = wqk.shape[1] // 2
    aw = wvt.shape[0]
    ch = wglu.shape[1] // 2
    ngate = wgate.shape[1]
    rows = PROJ_ROWS
    tiles = rows // ATTN_TILE
    blocks_per_seq = seq // rows
    row_spec = lambda w: pl.BlockSpec((rows, w), lambda i: (i, 0))
    return pl.pallas_call(
        functools.partial(_proj_kernel, d_qk=d_qk, kv_tiles_per_block=tiles),
        grid=(n // rows,),
        in_specs=[row_spec(d), _resident((1, d)), _resident(wqk.shape), _resident(wvt.shape),
                  _resident(wglu.shape), _resident(wgate.shape), _resident(bglu.shape),
                  _resident(bgate.shape)],
        out_specs=[row_spec(d_qk), row_spec(d_qk),
                   pl.BlockSpec((1, tiles, aw, ATTN_TILE),
                                lambda i: (i // blocks_per_seq, i % blocks_per_seq, 0, 0)),
                   row_spec(ch), row_spec(ngate)],
        out_shape=[jax.ShapeDtypeStruct((n, d_qk), _BF16), jax.ShapeDtypeStruct((n, d_qk), _BF16),
                   jax.ShapeDtypeStruct((batch, seq // ATTN_TILE, aw, ATTN_TILE), _BF16),
                   jax.ShapeDtypeStruct((n, ch), _F32), jax.ShapeDtypeStruct((n, ngate), _BF16)],
        compiler_params=_params("parallel"),
        name="proj",
    )(x2, g, wqk, wvt, wglu, wgate, bglu, bgate)


def _bucket_table():
    t, w = ATTN_TILE, ATTN_COLS
    r = np.arange(t, dtype=np.int64)[:, None]
    c = np.arange(w, dtype=np.int64)[None, :]
    rel = (t - w) + c - r
    n = np.maximum(rel, 0)
    max_exact = NUM_BUCKETS // 2
    nf = np.maximum(n, 1).astype(np.float32)
    large = max_exact + (np.log(nf / np.float32(max_exact)) / np.float32(math.log(MAX_DISTANCE / max_exact))
                         * np.float32(NUM_BUCKETS - max_exact)).astype(np.int32)
    large = np.minimum(large, NUM_BUCKETS - 1)
    return np.where(rel >= 0, np.where(n < max_exact, n, large), -1).astype(np.int32)


def _bias_kernel(rb_ref, bucket_ref, out_ref):
    h = pl.program_id(0)
    bucket = bucket_ref[...]
    far = rb_ref[NUM_BUCKETS - 1, h]
    acc = jnp.full(bucket.shape, MASKED, _F32)
    for b in range(NUM_BUCKETS):
        acc = jnp.where(bucket == b, (rb_ref[b, h] - far) * LOG2E, acc)
    out_ref[0] = acc


def _bias_tiles(rel_bias):
    nh = rel_bias.shape[1]
    t, w = ATTN_TILE, ATTN_COLS
    return pl.pallas_call(
        _bias_kernel,
        grid=(nh,),
        in_specs=[pl.BlockSpec(memory_space=pltpu.SMEM), _resident((t, w))],
        out_specs=pl.BlockSpec((1, t, w), lambda h: (h, 0, 0)),
        out_shape=jax.ShapeDtypeStruct((nh, t, w), _F32),
        compiler_params=_params("parallel"),
        name="bias_tiles",
    )(rel_bias, jnp.asarray(_bucket_table()))


MAX_CHAINS = 8
_PLAIN, _DIAG = 0, 1
_SUM_ROWS = 16


def _attn_steps(nq):
    steps = [[i, j, _DIAG if j == i else _PLAIN, 0] for i in range(nq) for j in range(i + 1)]
    for n in range(1, len(steps)):
        steps[n][3] = int(steps[n - 1][2] == _DIAG)
    return np.asarray(steps, dtype=np.int32).T.copy()


def _col_max(s):
    groups = [s[r:r + SUBLANES] for r in range(0, s.shape[0], SUBLANES)]
    acc = groups[:MAX_CHAINS]
    for n, g in enumerate(groups[MAX_CHAINS:]):
        acc[n % MAX_CHAINS] = jnp.maximum(acc[n % MAX_CHAINS], g)
    while len(acc) > 1:
        acc = [jnp.maximum(acc[n], acc[n + 1]) for n in range(0, len(acc), 2)]
    return jnp.max(acc[0], axis=0, keepdims=True)


def _tile_rows(idx, offset=0, size=ATTN_TILE):
    start = idx * ATTN_TILE + offset
    if not isinstance(start, int):
        start = pl.multiple_of(start, math.gcd(ATTN_TILE, offset))
    return pl.ds(start, size)


def _attn_kernel(tab_ref, q_ref, k_ref, vt_ref, bias_ref, lq1_ref, lk1_ref, lq2_ref, lk2_ref, g_ref,
                 o_ref, s0_ref, s1_ref, p0_ref, p1_ref, mt_ref, alpha_ref, m_ref, acc_ref, *, lambda_init, steps,
                 heads):
    t, w = ATTN_TILE, ATTN_COLS
    s_ref = (s0_ref, s1_ref)
    p_ref = (p0_ref, p1_ref)
    nsteps = steps.shape[1]
    parts = [(hh, a, c) for hh in range(heads) for a in range(2) for c in range(0, t, w)]

    def head_cols(hh):
        return slice(hh * V_HEAD_DIM, (hh + 1) * V_HEAD_DIM)

    def scores(i, j, slot, kind, hh, a, c):
        cb = c // w
        q = q_ref[_tile_rows(i, c, w), head_cols(hh)]
        kk = k_ref[_tile_rows(j), head_cols(hh)]
        lane = lax.broadcasted_iota(jnp.int32, q.shape, 1)
        keep = lane < HEAD_DIM if a == 0 else lane >= HEAD_DIM
        s = _dot_nt(kk, jnp.where(keep, q, jnp.zeros_like(q)))
        if kind == _DIAG:
            bias = bias_ref[hh, t - w - c:, :]
            if c + w < t:
                bias = jnp.concatenate([bias, jnp.full((t - w - c, w), MASKED, _F32)], axis=0)
            s = s + bias
        elif c == 0:
            near = jnp.where(j == i - 1, 1.0, 0.0).astype(_F32)
            corner = s[t - MAX_DISTANCE:, :MAX_DISTANCE] + (
                bias_ref[hh, t - w - MAX_DISTANCE:t - w, :MAX_DISTANCE] * near)
            bottom = jnp.concatenate([corner, s[t - MAX_DISTANCE:, MAX_DISTANCE:]], axis=1)
            s = jnp.concatenate([s[:t - MAX_DISTANCE], bottom], axis=0)
        s_ref[slot][hh, a, cb] = s
        mt_ref[slot, hh, a, cb] = _col_max(s)

    def softmax(i, j, slot, hh, a, c):
        cb = c // w
        m_old = jnp.where(j == 0, -jnp.inf, m_ref[hh, a, cb])
        m_new = jnp.maximum(m_old, mt_ref[slot, hh, a, cb])
        alpha_ref[slot, hh, a, cb] = jnp.exp2(m_old - m_new)
        p_ref[slot][hh, a, cb] = jnp.exp2(s_ref[slot][hh, a, cb] - m_new).astype(_BF16)
        m_ref[hh, a, cb] = m_new

    def values(i, j, slot, hh, a, c):
        cb = c // w
        vt = jnp.concatenate([vt_ref[0, j, head_cols(hh), :], jnp.ones((_SUM_ROWS, t), _BF16)], axis=0)
        acc_ref[i % 2, hh, a, cb] = (alpha_ref[slot, hh, a, cb] * acc_ref[i % 2, hh, a, cb]
                                     + _dot(vt, p_ref[slot][hh, a, cb]))

    def finalize(i, hh, cb):
        lam = (jnp.exp(jnp.sum(lq1_ref[...] * lk1_ref[...])) - jnp.exp(jnp.sum(lq2_ref[...] * lk2_ref[...]))
               + lambda_init)
        a1 = acc_ref[i % 2, hh, 0, cb]
        a2 = acc_ref[i % 2, hh, 1, cb]
        r1 = 1.0 / a1[V_HEAD_DIM:V_HEAD_DIM + 1]
        r2 = lam / a2[V_HEAD_DIM:V_HEAD_DIM + 1]
        o = a1[:V_HEAD_DIM] * r1 - a2[:V_HEAD_DIM] * r2
        o = o * lax.rsqrt(jnp.mean(o * o, axis=0, keepdims=True) + EPS)
        o = o * (g_ref[...] * (1.0 - lambda_init))
        o_ref[_tile_rows(i, cb * w, w), head_cols(hh)] = o.T

    fin_pieces = [(hh, cb) for hh in range(heads) for cb in range(t // w)]

    def pipeline_step(sc, sm, pv, fin):
        pending = list(fin_pieces) if fin is not None else []
        stride = max(len(parts) // max(len(fin_pieces), 1), 1)
        for n, (hh, a, c) in enumerate(parts):
            if sc is not None:
                scores(sc[0], sc[1], sc[3], sc[2], hh, a, c)
            if pv is not None:
                values(pv[0], pv[1], pv[3], hh, a, c)
            if sm is not None:
                softmax(sm[0], sm[1], sm[3], hh, a, c)
            if pending and n % stride == stride - 1:
                finalize(fin, *pending.pop(0))
        while pending:
            finalize(fin, *pending.pop(0))

    def static_step(n):
        return int(steps[0, n]), int(steps[1, n]), int(steps[2, n]), n % 2

    def static_fin(n):
        return int(steps[0, n - 1]) if n >= 1 and steps[3, n] else None

    acc_ref[...] = jnp.zeros(acc_ref.shape, _F32)
    m_ref[...] = jnp.full(m_ref.shape, -jnp.inf, _F32)

    pipeline_step(static_step(0), None, None, None)
    pipeline_step(static_step(1), static_step(0), None, None)

    full = nsteps - 2
    combos = sorted({(int(steps[2, n + 2]), int(steps[3, n])) for n in range(full)})

    def iteration(n, par):
        k_sc = tab_ref[2, n + 2]
        k_fin = tab_ref[3, n]
        for c_sc, c_fin in combos:
            @pl.when((k_sc == c_sc) & (k_fin == c_fin))
            def _():
                pipeline_step((tab_ref[0, n + 2], tab_ref[1, n + 2], c_sc, par),
                              (tab_ref[0, n + 1], tab_ref[1, n + 1], None, 1 - par),
                              (tab_ref[0, n], tab_ref[1, n], None, par),
                              tab_ref[0, jnp.maximum(n - 1, 0)] if c_fin else None)

    def body(pair, carry):
        iteration(2 * pair, 0)
        iteration(2 * pair + 1, 1)
        return carry

    lax.fori_loop(0, full // 2, body, 0)
    if full % 2:
        iteration(full - 1, (full - 1) % 2)

    pipeline_step(None, static_step(nsteps - 1), static_step(nsteps - 2), static_fin(nsteps - 2))
    pipeline_step(None, None, static_step(nsteps - 1), static_fin(nsteps - 1))
    pipeline_step(None, None, None, int(steps[0, nsteps - 1]))


def _attention(q, k, vt, bias, lq1, lk1, lq2, lk2, subln_g, *, batch, seq, lambda_init):
    n, d_qk = q.shape
    t = ATTN_TILE
    nq = seq // t
    nh = d_qk // V_HEAD_DIM
    hb = ATTN_HEADS
    aw = vt.shape[2]
    w, ncb = ATTN_COLS, t // ATTN_COLS
    steps = _attn_steps(nq)
    assert steps.shape[1] >= 3 and nh % hb == 0
    vec = _resident((1, HEAD_DIM))
    head_rows = pl.BlockSpec((seq, hb * V_HEAD_DIM), lambda b, h: (b, h))
    return pl.pallas_call(
        functools.partial(_attn_kernel, lambda_init=lambda_init, steps=steps, heads=hb),
        grid=(batch, nh // hb),
        in_specs=[pl.BlockSpec(memory_space=pltpu.SMEM), head_rows, head_rows,
                  pl.BlockSpec((1, nq, hb * V_HEAD_DIM, t), lambda b, h: (b, 0, h, 0)),
                  pl.BlockSpec((hb, t, w), lambda b, h: (h, 0, 0)),
                  vec, vec, vec, vec, _resident((V_HEAD_DIM, 1))],
        out_specs=head_rows,
        out_shape=jax.ShapeDtypeStruct((n, aw), _F32),
        scratch_shapes=[pltpu.VMEM((hb, 2, ncb, t, w), _F32), pltpu.VMEM((hb, 2, ncb, t, w), _F32),
                        pltpu.VMEM((hb, 2, ncb, t, w), _BF16), pltpu.VMEM((hb, 2, ncb, t, w), _BF16),
                        pltpu.VMEM((2, hb, 2, ncb, 1, w), _F32), pltpu.VMEM((2, hb, 2, ncb, 1, w), _F32),
                        pltpu.VMEM((hb, 2, ncb, 1, w), _F32),
                        pltpu.VMEM((2, hb, 2, ncb, V_HEAD_DIM + _SUM_ROWS, w), _F32)],
        compiler_params=_params("parallel", "parallel"),
        name="diff_attn",
    )(jnp.asarray(steps), q, k, vt, bias, lq1, lk1, lq2, lk2, subln_g)


def _zero_after(value, zero_ref):
    bits = lax.bitcast_convert_type(value, jnp.uint32) & zero_ref[...]
    return lax.bitcast_convert_type(bits, _F32)


def _branch_mlp_kernel(x_ref, attn_ref, gate_ref, c_ref, halo_ref, wb_ref, zero_ref, lng_ref, lnb_ref,
                       wpw_ref, bpw_ref, wout_ref, g2_ref, wup_ref, wdown_ref, gf_ref, o_ref,
                       y_ref, pad_ref, sh_ref, *, final_norm, tiles_per_seq):
    s = pl.program_id(0)
    rows, d = x_ref.shape
    first_of_seq = s % tiles_per_seq == 0
    lead = CONV_HALO - (CONV_WIDTH - 1)
    span = CONV_SUB + CONV_HALO - SUBLANES
    groups = CONV_CHUNK // SUBLANES
    nchunks = rows // CONV_CHUNK
    dff = wup_ref.shape[1]
    ndots = 2 + 2 * (dff // d)

    @pl.when(s == 0)
    def _():
        y_ref[...] = jnp.zeros(y_ref.shape, _F32)

    lane_blocks = y_ref.shape[1] // CONV_LANES
    nunits = nchunks * lane_blocks

    def conv_unit(q, zero):
        k, lb = divmod(q, lane_blocks)
        lanes = slice(lb * CONV_LANES, (lb + 1) * CONV_LANES)
        sub, r0 = divmod(k * CONV_CHUNK, CONV_SUB)
        base = sub * CONV_SUB
        if r0 == 0:
            if sub == 0:
                halo = halo_ref[:, lanes]
                pad_ref[:CONV_HALO, lanes] = jnp.where(first_of_seq, jnp.zeros_like(halo), halo)
            else:
                pad_ref[:CONV_HALO, lanes] = c_ref[base - CONV_HALO:base, lanes]
            body = c_ref[base:base + CONV_SUB, lanes]
            if zero is not None:
                body = (body.reshape(CONV_SUB // SUBLANES, SUBLANES, -1) + zero[None]).reshape(body.shape)
            pad_ref[CONV_HALO:, lanes] = body
            for b in range(1, SUBLANES):
                sh_ref[b - 1, :span, lanes] = pad_ref[b:b + span, lanes]
        bias = wb_ref[CONV_WIDTH, :, lanes]
        if zero is not None:
            bias = bias + zero
        acc = [bias] * groups
        for kt in range(CONV_WIDTH):
            a, b = divmod(kt + lead, SUBLANES)
            wv = wb_ref[kt, :, lanes]
            for g in range(groups):
                r = r0 + SUBLANES * (a + g)
                xs = pad_ref[r:r + SUBLANES, lanes] if b == 0 else sh_ref[b - 1, r:r + SUBLANES, lanes]
                acc[g] = acc[g] + xs * wv
        for g in range(groups):
            r = base + r0 + SUBLANES * g
            y_ref[r:r + SUBLANES, lanes] = acc[g]

    anchors = {}
    for q in range(1, nunits):
        pos = (q - 1) * ndots / nunits
        dot_idx = int(pos)
        row = int((pos - dot_idx) * rows) // SUBLANES * SUBLANES
        anchors.setdefault(dot_idx, []).append((q, row))

    def after_dot(dot_idx, result):
        for q, row in anchors.get(dot_idx, []):
            lb = q % lane_blocks
            piece = result[row:row + SUBLANES, lb * CONV_LANES:(lb + 1) * CONV_LANES]
            conv_unit(q, _zero_after(piece, zero_ref))

    y = y_ref[...]
    mu = jnp.mean(y, axis=-1, keepdims=True)
    yc = y - mu
    var = jnp.mean(yc * yc, axis=-1, keepdims=True)
    z = yc * lax.rsqrt(var + EPS) * lng_ref[...] + lnb_ref[...]
    z = (z * jax.nn.sigmoid(z)).astype(_BF16)
    conv_unit(0, None)

    pw = _dot(z, wpw_ref[...])
    after_dot(0, pw)
    gates = gate_ref[...].astype(_F32)
    merged = gates[:, :d] * attn_ref[...] + gates[:, d:] * (pw + bpw_ref[...])
    out = _dot(merged.astype(_BF16), wout_ref[...])
    after_dot(1, out)
    h1 = x_ref[...] + out
    u = (_rms(h1) * g2_ref[...]).astype(_BF16)
    acc = jnp.zeros(h1.shape, _F32)
    for n, c0 in enumerate(range(0, dff, d)):
        up = _dot(u, wup_ref[:, c0:c0 + d])
        after_dot(2 + 2 * n, up)
        hid = jnp.maximum(up, 0.0)
        down = _dot((hid * hid).astype(_BF16), wdown_ref[c0:c0 + d, :])
        after_dot(3 + 2 * n, down)
        acc = acc + down
    h2 = h1 + acc
    if final_norm:
        h2 = _rms(h2) * gf_ref[...]
    o_ref[...] = h2


def _branch_mlp(x2, attn, gates, c, wb, ln_g, ln_b, wpw, bpw, wout, g2, wup, wdown, gf, *, seq, final_norm):
    n, d = x2.shape
    ch = c.shape[1]
    rows = MLP_ROWS
    nt = n // rows
    halo_per_tile = rows // CONV_HALO
    prev = lambda w: pl.BlockSpec((rows, w), lambda s: (jnp.maximum(s - 1, 0), 0))
    cur = lambda s: jnp.minimum(s, nt - 1)
    return pl.pallas_call(
        functools.partial(_branch_mlp_kernel, final_norm=final_norm, tiles_per_seq=seq // rows),
        grid=(nt + 1,),
        in_specs=[prev(d), prev(d), prev(2 * d),
                  pl.BlockSpec((rows, ch), lambda s: (cur(s), 0)),
                  pl.BlockSpec((CONV_HALO, ch), lambda s: (jnp.maximum(cur(s) * halo_per_tile - 1, 0), 0)),
                  _resident(wb.shape), _resident((SUBLANES, CONV_LANES)), _resident((1, ch)), _resident((1, ch)),
                  _resident(wpw.shape),
                  _resident((1, d)), _resident(wout.shape), _resident((1, d)), _resident(wup.shape),
                  _resident(wdown.shape), _resident((1, d))],
        out_specs=prev(d),
        out_shape=jax.ShapeDtypeStruct((n, d), _F32),
        scratch_shapes=[pltpu.VMEM((rows, ch), _F32),
                        pltpu.VMEM((CONV_SUB + CONV_HALO, ch), _F32),
                        pltpu.VMEM((SUBLANES - 1, CONV_SUB + CONV_HALO, ch), _F32)],
        compiler_params=_params("arbitrary"),
        name="branch_mlp",
    )(x2, attn, gates, c, c, wb, jnp.zeros((SUBLANES, CONV_LANES), jnp.uint32), ln_g, ln_b, wpw, bpw, wout, g2, wup,
      wdown, gf)


def kernel(x, rel_bias, final_norm_g, norm_mix_g, w_in, b_glu, b_gate, lam_q1, lam_k1, lam_q2, lam_k2,
           subln_g, conv_w, conv_b, conv_ln_g, conv_ln_b, w_pw2, b_pw2, w_out, norm_mlp_g, w_up, w_down):
    batch, seq, d = x.shape
    depth = w_in.shape[0]
    ch = conv_w.shape[2]
    d_qk = (w_in.shape[2] - 2 * ch - 2 * d) // 3
    c1, c2, c3 = 2 * d_qk, 3 * d_qk, 3 * d_qk + 2 * ch
    assert seq % ATTN_TILE == 0 and seq % PROJ_ROWS == 0 and PROJ_ROWS % ATTN_TILE == 0
    assert seq % MLP_ROWS == 0 and MLP_ROWS % CONV_SUB == 0 and ATTN_TILE >= MAX_DISTANCE

    row = lambda v: v.reshape(1, -1)
    h = x.reshape(batch * seq, d)
    bias = _bias_tiles(rel_bias)
    for l in range(depth):
        lambda_init = 0.8 - 0.6 * math.exp(-0.3 * l)
        w = w_in[l]
        q, k, vt, c, gates = _proj(
            h, row(norm_mix_g[l]), w[:, :c1].astype(_BF16), w[:, c1:c2].T.astype(_BF16),
            w[:, c2:c3].astype(_BF16), w[:, c3:].astype(_BF16), row(b_glu[l]), row(b_gate[l]),
            batch=batch, seq=seq)
        attn = _attention(q, k, vt, bias, row(lam_q1[l]), row(lam_k1[l]), row(lam_q2[l]), row(lam_k2[l]),
                          subln_g[l].reshape(-1, 1), batch=batch, seq=seq, lambda_init=lambda_init)
        wb = jnp.concatenate([conv_w[l], conv_b[l][None]], axis=0)
        wb = jnp.broadcast_to(wb[:, None, :], (CONV_WIDTH + 1, SUBLANES, ch))
        h = _branch_mlp(h, attn, gates, c, wb, row(conv_ln_g[l]), row(conv_ln_b[l]), w_pw2[l].astype(_BF16),
                        row(b_pw2[l]), w_out[l].astype(_BF16), row(norm_mlp_g[l]), w_up[l].astype(_BF16),
                        w_down[l].astype(_BF16), row(final_norm_g), seq=seq, final_norm=(l == depth - 1))
    return h.reshape(batch, seq, d)
```

```python
import functools
import math

import jax
import jax.numpy as jnp
import numpy as np
from jax import lax
from jax.experimental import pallas as pl
from jax.experimental.pallas import tpu as pltpu

HEAD_DIM = 64
V_HEAD_DIM = 2 * HEAD_DIM
CONV_WIDTH = 31
NUM_BUCKETS = 32
MAX_DISTANCE = 128
EPS = 1e-6
MASKED = -1e30
LOG2E = 1.4426950408889634

PROJ_ROWS = 512
ATTN_TILE = 512
ATTN_COLS = 256
ATTN_HEADS = 2
MLP_ROWS = 512
CONV_SUB = 128
CONV_HALO = 32
CONV_CHUNK = 16
CONV_LANES = 1024
SUBLANES = 8
VMEM_LIMIT = 56 * 1024 * 1024

_BF16 = jnp.bfloat16
_F32 = jnp.float32


def _resident(shape):
    return pl.BlockSpec(shape, lambda *_: (0,) * len(shape), pipeline_mode=pl.Buffered(1))


def _params(*semantics):
    return pltpu.CompilerParams(dimension_semantics=semantics, vmem_limit_bytes=VMEM_LIMIT)


def _rms(xf):
    return xf * lax.rsqrt(jnp.mean(xf * xf, axis=-1, keepdims=True) + EPS)


def _dot(a, b):
    return jnp.dot(a, b, preferred_element_type=_F32)


def _dot_nt(a, b):
    return lax.dot_general(a, b, (((1,), (1,)), ((), ())), preferred_element_type=_F32)


def _proj_kernel(x_ref, g_ref, wqk_ref, wvt_ref, wglu_ref, wgate_ref, bglu_ref, bgate_ref,
                 q_ref, k_ref, vt_ref, c_ref, gate_ref, *, d_qk, kv_tiles_per_block):
    u = (_rms(x_ref[...]) * g_ref[...]).astype(_BF16)
    ch = c_ref.shape[-1]
    for g0 in range(0, gate_ref.shape[-1], ch):
        gate_ref[:, g0:g0 + ch] = jax.nn.sigmoid(
            _dot(u, wgate_ref[:, g0:g0 + ch]) + bgate_ref[:, g0:g0 + ch]).astype(_BF16)
    ga = _dot(u, wglu_ref[:, :ch]) + bglu_ref[:, :ch]
    gb = _dot(u, wglu_ref[:, ch:]) + bglu_ref[:, ch:]
    c_ref[...] = ga * jax.nn.sigmoid(gb)
    scale = LOG2E / math.sqrt(HEAD_DIM)
    q_ref[...] = (_dot(u, wqk_ref[:, :d_qk]) * scale).astype(_BF16)
    vt = _dot_nt(wvt_ref[...], u).astype(_BF16)
    for t in range(kv_tiles_per_block):
        vt_ref[0, t] = vt[:, t * ATTN_TILE:(t + 1) * ATTN_TILE]
    k_ref[...] = _dot(u, wqk_ref[:, d_qk:]).astype(_BF16)


def _proj(x2, g, wqk, wvt, wglu, wgate, bglu, bgate, *, batch, seq):
    n, d = x2.shape
    d_qk = wqk.shape[1] // 2
    aw = wvt.shape[0]
    ch = wglu.shape[1] // 2
    ngate = wgate.shape[1]
    rows = PROJ_ROWS
    tiles = rows // ATTN_TILE
    blocks_per_seq = seq // rows
    row_spec = lambda w: pl.BlockSpec((rows, w), lambda i: (i, 0))
    return pl.pallas_call(
        functools.partial(_proj_kernel, d_qk=d_qk, kv_tiles_per_block=tiles),
        grid=(n // rows,),
        in_specs=[row_spec(d), _resident((1, d)), _resident(wqk.shape), _resident(wvt.shape),
                  _resident(wglu.shape), _resident(wgate.shape), _resident(bglu.shape),
                  _resident(bgate.shape)],
        out_specs=[row_spec(d_qk), row_spec(d_qk),
                   pl.BlockSpec((1, tiles, aw, ATTN_TILE),
                                lambda i: (i // blocks_per_seq, i % blocks_per_seq, 0, 0)),
                   row_spec(ch), row_spec(ngate)],
        out_shape=[jax.ShapeDtypeStruct((n, d_qk), _BF16), jax.ShapeDtypeStruct((n, d_qk), _BF16),
                   jax.ShapeDtypeStruct((batch, seq // ATTN_TILE, aw, ATTN_TILE), _BF16),
                   jax.ShapeDtypeStruct((n, ch), _F32), jax.ShapeDtypeStruct((n, ngate), _BF16)],
        compiler_params=_params("parallel"),
        name="proj",
    )(x2, g, wqk, wvt, wglu, wgate, bglu, bgate)


def _bucket_table():
    t, w = ATTN_TILE, ATTN_COLS
    r = np.arange(t, dtype=np.int64)[:, None]
    c = np.arange(w, dtype=np.int64)[None, :]
    rel = (t - w) + c - r
    n = np.maximum(rel, 0)
    max_exact = NUM_BUCKETS // 2
    nf = np.maximum(n, 1).astype(np.float32)
    large = max_exact + (np.log(nf / np.float32(max_exact)) / np.float32(math.log(MAX_DISTANCE / max_exact))
                         * np.float32(NUM_BUCKETS - max_exact)).astype(np.int32)
    large = np.minimum(large, NUM_BUCKETS - 1)
    return np.where(rel >= 0, np.where(n < max_exact, n, large), -1).astype(np.int32)


def _bias_kernel(rb_ref, bucket_ref, out_ref):
    h = pl.program_id(0)
    bucket = bucket_ref[...]
    far = rb_ref[NUM_BUCKETS - 1, h]
    acc = jnp.full(bucket.shape, MASKED, _F32)
    for b in range(NUM_BUCKETS):
        acc = jnp.where(bucket == b, (rb_ref[b, h] - far) * LOG2E, acc)
    out_ref[0] = acc


def _bias_tiles(rel_bias):
    nh = rel_bias.shape[1]
    t, w = ATTN_TILE, ATTN_COLS
    return pl.pallas_call(
        _bias_kernel,
        grid=(nh,),
        in_specs=[pl.BlockSpec(memory_space=pltpu.SMEM), _resident((t, w))],
        out_specs=pl.BlockSpec((1, t, w), lambda h: (h, 0, 0)),
        out_shape=jax.ShapeDtypeStruct((nh, t, w), _F32),
        compiler_params=_params("parallel"),
        name="bias_tiles",
    )(rel_bias, jnp.asarray(_bucket_table()))


MAX_CHAINS = 8
_PLAIN, _DIAG = 0, 1
_SUM_ROWS = 16


def _attn_steps(nq):
    steps = [[i, j, _DIAG if j == i else _PLAIN, 0] for i in range(nq) for j in range(i + 1)]
    for n in range(1, len(steps)):
        steps[n][3] = int(steps[n - 1][2] == _DIAG)
    return np.asarray(steps, dtype=np.int32).T.copy()


def _col_max(s):
    groups = [s[r:r + SUBLANES] for r in range(0, s.shape[0], SUBLANES)]
    acc = groups[:MAX_CHAINS]
    for n, g in enumerate(groups[MAX_CHAINS:]):
        acc[n % MAX_CHAINS] = jnp.maximum(acc[n % MAX_CHAINS], g)
    while len(acc) > 1:
        acc = [jnp.maximum(acc[n], acc[n + 1]) for n in range(0, len(acc), 2)]
    return jnp.max(acc[0], axis=0, keepdims=True)


def _tile_rows(idx, offset=0, size=ATTN_TILE):
    start = idx * ATTN_TILE + offset
    if not isinstance(start, int):
        start = pl.multiple_of(start, math.gcd(ATTN_TILE, offset))
    return pl.ds(start, size)


def _attn_kernel(tab_ref, q_ref, k_ref, vt_ref, bias_ref, lq1_ref, lk1_ref, lq2_ref, lk2_ref, g_ref,
                 o_ref, s0_ref, s1_ref, mt_ref, m_ref, acc_ref, *, lambda_init, steps, heads):
    t, w = ATTN_TILE, ATTN_COLS
    s_ref = (s0_ref, s1_ref)
    nsteps = steps.shape[1]
    parts = [(hh, a, c) for hh in range(heads) for a in range(2) for c in range(0, t, w)]

    def head_cols(hh):
        return slice(hh * V_HEAD_DIM, (hh + 1) * V_HEAD_DIM)

    def scores(i, j, slot, kind, hh, a, c):
        cb = c // w
        q = q_ref[_tile_rows(i, c, w), head_cols(hh)]
        kk = k_ref[_tile_rows(j), head_cols(hh)]
        lane = lax.broadcasted_iota(jnp.int32, q.shape, 1)
        keep = lane < HEAD_DIM if a == 0 else lane >= HEAD_DIM
        s = _dot_nt(kk, jnp.where(keep, q, jnp.zeros_like(q)))
        if kind == _DIAG:
            bias = bias_ref[hh, t - w - c:, :]
            if c + w < t:
                bias = jnp.concatenate([bias, jnp.full((t - w - c, w), MASKED, _F32)], axis=0)
            s = s + bias
        elif c == 0:
            near = jnp.where(j == i - 1, 1.0, 0.0).astype(_F32)
            corner = s[t - MAX_DISTANCE:, :MAX_DISTANCE] + (
                bias_ref[hh, t - w - MAX_DISTANCE:t - w, :MAX_DISTANCE] * near)
            bottom = jnp.concatenate([corner, s[t - MAX_DISTANCE:, MAX_DISTANCE:]], axis=1)
            s = jnp.concatenate([s[:t - MAX_DISTANCE], bottom], axis=0)
        s_ref[slot][hh, a, cb] = s
        mt_ref[slot, hh, a, cb] = _col_max(s)

    def softmax_values(i, j, slot, hh, a, c):
        cb = c // w
        m_old = jnp.where(j == 0, -jnp.inf, m_ref[hh, a, cb])
        m_new = jnp.maximum(m_old, mt_ref[slot, hh, a, cb])
        alpha = jnp.exp2(m_old - m_new)
        p = jnp.exp2(s_ref[slot][hh, a, cb] - m_new).astype(_BF16)
        m_ref[hh, a, cb] = m_new
        vt = jnp.concatenate([vt_ref[0, j, head_cols(hh), :], jnp.ones((_SUM_ROWS, t), _BF16)], axis=0)
        acc_ref[i % 2, hh, a, cb] = alpha * acc_ref[i % 2, hh, a, cb] + _dot(vt, p)

    def finalize(i, hh, cb):
        lam = (jnp.exp(jnp.sum(lq1_ref[...] * lk1_ref[...])) - jnp.exp(jnp.sum(lq2_ref[...] * lk2_ref[...]))
               + lambda_init)
        a1 = acc_ref[i % 2, hh, 0, cb]
        a2 = acc_ref[i % 2, hh, 1, cb]
        r1 = 1.0 / a1[V_HEAD_DIM:V_HEAD_DIM + 1]
        r2 = lam / a2[V_HEAD_DIM:V_HEAD_DIM + 1]
        o = a1[:V_HEAD_DIM] * r1 - a2[:V_HEAD_DIM] * r2
        o = o * lax.rsqrt(jnp.mean(o * o, axis=0, keepdims=True) + EPS)
        o = o * (g_ref[...] * (1.0 - lambda_init))
        o_ref[_tile_rows(i, cb * w, w), head_cols(hh)] = o.T

    fin_pieces = [(hh, cb) for hh in range(heads) for cb in range(t // w)]

    def pipeline_step(sc, sv, fin):
        pending = list(fin_pieces) if fin is not None else []
        stride = max(len(parts) // max(len(fin_pieces), 1), 1)
        for n, (hh, a, c) in enumerate(parts):
            if sc is not None:
                scores(sc[0], sc[1], sc[3], sc[2], hh, a, c)
            if sv is not None:
                softmax_values(sv[0], sv[1], sv[3], hh, a, c)
            if pending and n % stride == stride - 1:
                finalize(fin, *pending.pop(0))
        while pending:
            finalize(fin, *pending.pop(0))

    def static_step(n):
        return int(steps[0, n]), int(steps[1, n]), int(steps[2, n]), n % 2

    def static_fin(n):
        return int(steps[0, n - 1]) if n >= 1 and steps[3, n] else None

    acc_ref[...] = jnp.zeros(acc_ref.shape, _F32)
    m_ref[...] = jnp.full(m_ref.shape, -jnp.inf, _F32)

    pipeline_step(static_step(0), None, None)

    full = nsteps - 1
    combos = sorted({(int(steps[2, n + 1]), int(steps[3, n])) for n in range(full)})

    def iteration(n, par):
        k_sc = tab_ref[2, n + 1]
        k_fin = tab_ref[3, n]
        for c_sc, c_fin in combos:
            @pl.when((k_sc == c_sc) & (k_fin == c_fin))
            def _():
                pipeline_step((tab_ref[0, n + 1], tab_ref[1, n + 1], c_sc, 1 - par),
                              (tab_ref[0, n], tab_ref[1, n], None, par),
                              tab_ref[0, jnp.maximum(n - 1, 0)] if c_fin else None)

    def body(pair, carry):
        iteration(2 * pair, 0)
        iteration(2 * pair + 1, 1)
        return carry

    lax.fori_loop(0, full // 2, body, 0)
    if full % 2:
        iteration(full - 1, (full - 1) % 2)

    pipeline_step(None, static_step(nsteps - 1), static_fin(nsteps - 1))
    pipeline_step(None, None, int(steps[0, nsteps - 1]))


def _attention(q, k, vt, bias, lq1, lk1, lq2, lk2, subln_g, *, batch, seq, lambda_init):
    n, d_qk = q.shape
    t = ATTN_TILE
    nq = seq // t
    nh = d_qk // V_HEAD_DIM
    hb = ATTN_HEADS
    aw = vt.shape[2]
    w, ncb = ATTN_COLS, t // ATTN_COLS
    steps = _attn_steps(nq)
    assert steps.shape[1] >= 3 and nh % hb == 0
    vec = _resident((1, HEAD_DIM))
    head_rows = pl.BlockSpec((seq, hb * V_HEAD_DIM), lambda b, h: (b, h))
    return pl.pallas_call(
        functools.partial(_attn_kernel, lambda_init=lambda_init, steps=steps, heads=hb),
        grid=(batch, nh // hb),
        in_specs=[pl.BlockSpec(memory_space=pltpu.SMEM), head_rows, head_rows,
                  pl.BlockSpec((1, nq, hb * V_HEAD_DIM, t), lambda b, h: (b, 0, h, 0)),
                  pl.BlockSpec((hb, t, w), lambda b, h: (h, 0, 0)),
                  vec, vec, vec, vec, _resident((V_HEAD_DIM, 1))],
        out_specs=head_rows,
        out_shape=jax.ShapeDtypeStruct((n, aw), _F32),
        scratch_shapes=[pltpu.VMEM((hb, 2, ncb, t, w), _F32), pltpu.VMEM((hb, 2, ncb, t, w), _F32),
                        pltpu.VMEM((2, hb, 2, ncb, 1, w), _F32), pltpu.VMEM((hb, 2, ncb, 1, w), _F32),
                        pltpu.VMEM((2, hb, 2, ncb, V_HEAD_DIM + _SUM_ROWS, w), _F32)],
        compiler_params=_params("parallel", "parallel"),
        name="diff_attn",
    )(jnp.asarray(steps), q, k, vt, bias, lq1, lk1, lq2, lk2, subln_g)


def _zero_after(value, zero_ref):
    bits = lax.bitcast_convert_type(value, jnp.uint32) & zero_ref[...]
    return lax.bitcast_convert_type(bits, _F32)


def _branch_mlp_kernel(x_ref, attn_ref, gate_ref, c_ref, halo_ref, wb_ref, zero_ref, lng_ref, lnb_ref,
                       wpw_ref, bpw_ref, wout_ref, g2_ref, wup_ref, wdown_ref, gf_ref, o_ref,
                       y_ref, pad_ref, sh_ref, *, final_norm, tiles_per_seq):
    s = pl.program_id(0)
    rows, d = x_ref.shape
    first_of_seq = s % tiles_per_seq == 0
    lead = CONV_HALO - (CONV_WIDTH - 1)
    span = CONV_SUB + CONV_HALO - SUBLANES
    groups = CONV_CHUNK // SUBLANES
    nchunks = rows // CONV_CHUNK
    dff = wup_ref.shape[1]
    ndots = 2 + 2 * (dff // d)

    @pl.when(s == 0)
    def _():
        y_ref[...] = jnp.zeros(y_ref.shape, _F32)

    lane_blocks = y_ref.shape[1] // CONV_LANES
    nunits = nchunks * lane_blocks

    def conv_unit(q, zero):
        k, lb = divmod(q, lane_blocks)
        lanes = slice(lb * CONV_LANES, (lb + 1) * CONV_LANES)
        sub, r0 = divmod(k * CONV_CHUNK, CONV_SUB)
        base = sub * CONV_SUB
        if r0 == 0:
            if sub == 0:
                halo = halo_ref[:, lanes]
                pad_ref[:CONV_HALO, lanes] = jnp.where(first_of_seq, jnp.zeros_like(halo), halo)
            else:
                pad_ref[:CONV_HALO, lanes] = c_ref[base - CONV_HALO:base, lanes]
            body = c_ref[base:base + CONV_SUB, lanes]
            if zero is not None:
                body = (body.reshape(CONV_SUB // SUBLANES, SUBLANES, -1) + zero[None]).reshape(body.shape)
            pad_ref[CONV_HALO:, lanes] = body
            for b in range(1, SUBLANES):
                sh_ref[b - 1, :span, lanes] = pad_ref[b:b + span, lanes]
        bias = wb_ref[CONV_WIDTH, :, lanes]
        if zero is not None:
            bias = bias + zero
        acc = [bias] * groups
        for kt in range(CONV_WIDTH):
            a, b = divmod(kt + lead, SUBLANES)
            wv = wb_ref[kt, :, lanes]
            for g in range(groups):
                r = r0 + SUBLANES * (a + g)
                xs = pad_ref[r:r + SUBLANES, lanes] if b == 0 else sh_ref[b - 1, r:r + SUBLANES, lanes]
                acc[g] = acc[g] + xs * wv
        for g in range(groups):
            r = base + r0 + SUBLANES * g
            y_ref[r:r + SUBLANES, lanes] = acc[g]

    anchors = {}
    for q in range(1, nunits):
        pos = (q - 1) * ndots / nunits
        dot_idx = int(pos)
        row = int((pos - dot_idx) * rows) // SUBLANES * SUBLANES
        anchors.setdefault(dot_idx, []).append((q, row))

    def after_dot(dot_idx, result):
        for q, row in anchors.get(dot_idx, []):
            lb = q % lane_blocks
            piece = result[row:row + SUBLANES, lb * CONV_LANES:(lb + 1) * CONV_LANES]
            conv_unit(q, _zero_after(piece, zero_ref))

    y = y_ref[...]
    mu = jnp.mean(y, axis=-1, keepdims=True)
    yc = y - mu
    var = jnp.mean(yc * yc, axis=-1, keepdims=True)
    z = yc * lax.rsqrt(var + EPS) * lng_ref[...] + lnb_ref[...]
    z = (z * jax.nn.sigmoid(z)).astype(_BF16)
    conv_unit(0, None)

    pw = _dot(z, wpw_ref[...])
    after_dot(0, pw)
    gates = gate_ref[...].astype(_F32)
    merged = gates[:, :d] * attn_ref[...] + gates[:, d:] * (pw + bpw_ref[...])
    out = _dot(merged.astype(_BF16), wout_ref[...])
    after_dot(1, out)
    h1 = x_ref[...] + out
    u = (_rms(h1) * g2_ref[...]).astype(_BF16)
    acc = jnp.zeros(h1.shape, _F32)
    for n, c0 in enumerate(range(0, dff, d)):
        up = _dot(u, wup_ref[:, c0:c0 + d])
        after_dot(2 + 2 * n, up)
        hid = jnp.maximum(up, 0.0)
        down = _dot((hid * hid).astype(_BF16), wdown_ref[c0:c0 + d, :])
        after_dot(3 + 2 * n, down)
        acc = acc + down
    h2 = h1 + acc
    if final_norm:
        h2 = _rms(h2) * gf_ref[...]
    o_ref[...] = h2


def _branch_mlp(x2, attn, gates, c, wb, ln_g, ln_b, wpw, bpw, wout, g2, wup, wdown, gf, *, seq, final_norm):
    n, d = x2.shape
    ch = c.shape[1]
    rows = MLP_ROWS
    nt = n // rows
    halo_per_tile = rows // CONV_HALO
    prev = lambda w: pl.BlockSpec((rows, w), lambda s: (jnp.maximum(s - 1, 0), 0))
    cur = lambda s: jnp.minimum(s, nt - 1)
    return pl.pallas_call(
        functools.partial(_branch_mlp_kernel, final_norm=final_norm, tiles_per_seq=seq // rows),
        grid=(nt + 1,),
        in_specs=[prev(d), prev(d), prev(2 * d),
                  pl.BlockSpec((rows, ch), lambda s: (cur(s), 0)),
                  pl.BlockSpec((CONV_HALO, ch), lambda s: (jnp.maximum(cur(s) * halo_per_tile - 1, 0), 0)),
                  _resident(wb.shape), _resident((SUBLANES, CONV_LANES)), _resident((1, ch)), _resident((1, ch)),
                  _resident(wpw.shape),
                  _resident((1, d)), _resident(wout.shape), _resident((1, d)), _resident(wup.shape),
                  _resident(wdown.shape), _resident((1, d))],
        out_specs=prev(d),
        out_shape=jax.ShapeDtypeStruct((n, d), _F32),
        scratch_shapes=[pltpu.VMEM((rows, ch), _F32),
                        pltpu.VMEM((CONV_SUB + CONV_HALO, ch), _F32),
                        pltpu.VMEM((SUBLANES - 1, CONV_SUB + CONV_HALO, ch), _F32)],
        compiler_params=_params("arbitrary"),
        name="branch_mlp",
    )(x2, attn, gates, c, c, wb, jnp.zeros((SUBLANES, CONV_LANES), jnp.uint32), ln_g, ln_b, wpw, bpw, wout, g2, wup,
      wdown, gf)


def kernel(x, rel_bias, final_norm_g, norm_mix_g, w_in, b_glu, b_gate, lam_q1, lam_k1, lam_q2, lam_k2,
           subln_g, conv_w, conv_b, conv_ln_g, conv_ln_b, w_pw2, b_pw2, w_out, norm_mlp_g, w_up, w_down):
    batch, seq, d = x.shape
    depth = w_in.shape[0]
    ch = conv_w.shape[2]
    d_qk = (w_in.shape[2] - 2 * ch - 2 * d) // 3
    c1, c2, c3 = 2 * d_qk, 3 * d_qk, 3 * d_qk + 2 * ch
    assert seq % ATTN_TILE == 0 and seq % PROJ_ROWS == 0 and PROJ_ROWS % ATTN_TILE == 0
    assert seq % MLP_ROWS == 0 and MLP_ROWS % CONV_SUB == 0 and ATTN_TILE >= MAX_DISTANCE

    row = lambda v: v.reshape(1, -1)
    h = x.reshape(batch * seq, d)
    bias = _bias_tiles(rel_bias)
    for l in range(depth):
        lambda_init = 0.8 - 0.6 * math.exp(-0.3 * l)
        w = w_in[l]
        q, k, vt, c, gates = _proj(
            h, row(norm_mix_g[l]), w[:, :c1].astype(_BF16), w[:, c1:c2].T.astype(_BF16),
            w[:, c2:c3].astype(_BF16), w[:, c3:].astype(_BF16), row(b_glu[l]), row(b_gate[l]),
            batch=batch, seq=seq)
        attn = _attention(q, k, vt, bias, row(lam_q1[l]), row(lam_k1[l]), row(lam_q2[l]), row(lam_k2[l]),
                          subln_g[l].reshape(-1, 1), batch=batch, seq=seq, lambda_init=lambda_init)
        wb = jnp.concatenate([conv_w[l], conv_b[l][None]], axis=0)
        wb = jnp.broadcast_to(wb[:, None, :], (CONV_WIDTH + 1, SUBLANES, ch))
        h = _branch_mlp(h, attn, gates, c, wb, row(conv_ln_g[l]), row(conv_ln_b[l]), w_pw2[l].astype(_BF16),
                        row(b_pw2[l]), w_out[l].astype(_BF16), row(norm_mlp_g[l]), w_up[l].astype(_BF16),
                        w_down[l].astype(_BF16), row(final_norm_g), seq=seq, final_norm=(l == depth - 1))
    return h.reshape(batch, seq, d)
```

```python
import functools
import math

import jax
import jax.numpy as jnp
import numpy as np
from jax import lax
from jax.experimental import pallas as pl
from jax.experimental.pallas import tpu as pltpu

HEAD_DIM = 64
V_HEAD_DIM = 2 * HEAD_DIM
CONV_WIDTH = 31
NUM_BUCKETS = 32
MAX_DISTANCE = 128
EPS = 1e-6
MASKED = -1e30
LOG2E = 1.4426950408889634

PROJ_ROWS = 512
ATTN_TILE = 512
ATTN_COLS = 256
ATTN_HEADS = 2
MLP_ROWS = 512
CONV_SUB = 128
CONV_HALO = 32
CONV_CHUNK = 64
CONV_LANES = 128
SUBLANES = 8
VMEM_LIMIT = 56 * 1024 * 1024

_BF16 = jnp.bfloat16
_F32 = jnp.float32


def _resident(shape):
    return pl.BlockSpec(shape, lambda *_: (0,) * len(shape), pipeline_mode=pl.Buffered(1))


def _params(*semantics):
    return pltpu.CompilerParams(dimension_semantics=semantics, vmem_limit_bytes=VMEM_LIMIT)


def _rms(xf):
    return xf * lax.rsqrt(jnp.mean(xf * xf, axis=-1, keepdims=True) + EPS)


def _dot(a, b):
    return jnp.dot(a, b, preferred_element_type=_F32)


def _dot_nt(a, b):
    return lax.dot_general(a, b, (((1,), (1,)), ((), ())), preferred_element_type=_F32)


def _proj_kernel(x_ref, g_ref, wqk_ref, wvt_ref, wglu_ref, wgate_ref, bglu_ref, bgate_ref,
                 q_ref, k_ref, vt_ref, c_ref, gate_ref, *, d_qk, kv_tiles_per_block):
    u = (_rms(x_ref[...]) * g_ref[...]).astype(_BF16)
    ch = c_ref.shape[-1]
    for g0 in range(0, gate_ref.shape[-1], ch):
        gate_ref[:, g0:g0 + ch] = jax.nn.sigmoid(
            _dot(u, wgate_ref[:, g0:g0 + ch]) + bgate_ref[:, g0:g0 + ch]).astype(_BF16)
    ga = _dot(u, wglu_ref[:, :ch]) + bglu_ref[:, :ch]
    gb = _dot(u, wglu_ref[:, ch:]) + bglu_ref[:, ch:]
    c_ref[...] = ga * jax.nn.sigmoid(gb)
    scale = LOG2E / math.sqrt(HEAD_DIM)
    q_ref[...] = (_dot(u, wqk_ref[:, :d_qk]) * scale).astype(_BF16)
    vt = _dot_nt(wvt_ref[...], u).astype(_BF16)
    for t in range(kv_tiles_per_block):
        vt_ref[0, t] = vt[:, t * ATTN_TILE:(t + 1) * ATTN_TILE]
    k_ref[...] = _dot(u, wqk_ref[:, d_qk:]).astype(_BF16)


def _proj(x2, g, wqk, wvt, wglu, wgate, bglu, bgate, *, batch, seq):
    n, d = x2.shape
    d_qk = wqk.shape[1] // 2
    aw = wvt.shape[0]
    ch = wglu.shape[1] // 2
    ngate = wgate.shape[1]
    rows = PROJ_ROWS
    tiles = rows // ATTN_TILE
    blocks_per_seq = seq // rows
    row_spec = lambda w: pl.BlockSpec((rows, w), lambda i: (i, 0))
    return pl.pallas_call(
        functools.partial(_proj_kernel, d_qk=d_qk, kv_tiles_per_block=tiles),
        grid=(n // rows,),
        in_specs=[row_spec(d), _resident((1, d)), _resident(wqk.shape), _resident(wvt.shape),
                  _resident(wglu.shape), _resident(wgate.shape), _resident(bglu.shape),
                  _resident(bgate.shape)],
        out_specs=[row_spec(d_qk), row_spec(d_qk),
                   pl.BlockSpec((1, tiles, aw, ATTN_TILE),
                                lambda i: (i // blocks_per_seq, i % blocks_per_seq, 0, 0)),
                   row_spec(ch), row_spec(ngate)],
        out_shape=[jax.ShapeDtypeStruct((n, d_qk), _BF16), jax.ShapeDtypeStruct((n, d_qk), _BF16),
                   jax.ShapeDtypeStruct((batch, seq // ATTN_TILE, aw, ATTN_TILE), _BF16),
                   jax.ShapeDtypeStruct((n, ch), _F32), jax.ShapeDtypeStruct((n, ngate), _BF16)],
        compiler_params=_params("parallel"),
        name="proj",
    )(x2, g, wqk, wvt, wglu, wgate, bglu, bgate)


def _bucket_table():
    t, w = ATTN_TILE, ATTN_COLS
    r = np.arange(t, dtype=np.int64)[:, None]
    c = np.arange(w, dtype=np.int64)[None, :]
    rel = (t - w) + c - r
    n = np.maximum(rel, 0)
    max_exact = NUM_BUCKETS // 2
    nf = np.maximum(n, 1).astype(np.float32)
    large = max_exact + (np.log(nf / np.float32(max_exact)) / np.float32(math.log(MAX_DISTANCE / max_exact))
                         * np.float32(NUM_BUCKETS - max_exact)).astype(np.int32)
    large = np.minimum(large, NUM_BUCKETS - 1)
    return np.where(rel >= 0, np.where(n < max_exact, n, large), -1).astype(np.int32)


def _bias_kernel(rb_ref, bucket_ref, out_ref):
    h = pl.program_id(0)
    bucket = bucket_ref[...]
    far = rb_ref[NUM_BUCKETS - 1, h]
    acc = jnp.full(bucket.shape, MASKED, _F32)
    for b in range(NUM_BUCKETS):
        acc = jnp.where(bucket == b, (rb_ref[b, h] - far) * LOG2E, acc)
    out_ref[0] = acc


def _bias_tiles(rel_bias):
    nh = rel_bias.shape[1]
    t, w = ATTN_TILE, ATTN_COLS
    return pl.pallas_call(
        _bias_kernel,
        grid=(nh,),
        in_specs=[pl.BlockSpec(memory_space=pltpu.SMEM), _resident((t, w))],
        out_specs=pl.BlockSpec((1, t, w), lambda h: (h, 0, 0)),
        out_shape=jax.ShapeDtypeStruct((nh, t, w), _F32),
        compiler_params=_params("parallel"),
        name="bias_tiles",
    )(rel_bias, jnp.asarray(_bucket_table()))


MAX_CHAINS = 8
_PLAIN, _DIAG = 0, 1
_SUM_ROWS = 16


def _attn_steps(nq):
    steps = [[i, j, _DIAG if j == i else _PLAIN, 0] for i in range(nq) for j in range(i + 1)]
    for n in range(1, len(steps)):
        steps[n][3] = int(steps[n - 1][2] == _DIAG)
    return np.asarray(steps, dtype=np.int32).T.copy()


def _col_max(s):
    groups = [s[r:r + SUBLANES] for r in range(0, s.shape[0], SUBLANES)]
    acc = groups[:MAX_CHAINS]
    for n, g in enumerate(groups[MAX_CHAINS:]):
        acc[n % MAX_CHAINS] = jnp.maximum(acc[n % MAX_CHAINS], g)
    while len(acc) > 1:
        acc = [jnp.maximum(acc[n], acc[n + 1]) for n in range(0, len(acc), 2)]
    return jnp.max(acc[0], axis=0, keepdims=True)


def _tile_rows(idx, offset=0, size=ATTN_TILE):
    start = idx * ATTN_TILE + offset
    if not isinstance(start, int):
        start = pl.multiple_of(start, math.gcd(ATTN_TILE, offset))
    return pl.ds(start, size)


def _attn_kernel(tab_ref, q_ref, k_ref, vt_ref, bias_ref, lq1_ref, lk1_ref, lq2_ref, lk2_ref, g_ref,
                 o_ref, s0_ref, s1_ref, mt_ref, m_ref, acc_ref, *, lambda_init, steps, heads):
    t, w = ATTN_TILE, ATTN_COLS
    s_ref = (s0_ref, s1_ref)
    nsteps = steps.shape[1]
    parts = [(hh, a, c) for hh in range(heads) for a in range(2) for c in range(0, t, w)]

    def head_cols(hh):
        return slice(hh * V_HEAD_DIM, (hh + 1) * V_HEAD_DIM)

    def scores(i, j, slot, kind, hh, a, c):
        cb = c // w
        q = q_ref[_tile_rows(i, c, w), head_cols(hh)]
        kk = k_ref[_tile_rows(j), head_cols(hh)]
        lane = lax.broadcasted_iota(jnp.int32, q.shape, 1)
        keep = lane < HEAD_DIM if a == 0 else lane >= HEAD_DIM
        s = _dot_nt(kk, jnp.where(keep, q, jnp.zeros_like(q)))
        if kind == _DIAG:
            bias = bias_ref[hh, t - w - c:, :]
            if c + w < t:
                bias = jnp.concatenate([bias, jnp.full((t - w - c, w), MASKED, _F32)], axis=0)
            s = s + bias
        elif c == 0:
            near = jnp.where(j == i - 1, 1.0, 0.0).astype(_F32)
            corner = s[t - MAX_DISTANCE:, :MAX_DISTANCE] + (
                bias_ref[hh, t - w - MAX_DISTANCE:t - w, :MAX_DISTANCE] * near)
            bottom = jnp.concatenate([corner, s[t - MAX_DISTANCE:, MAX_DISTANCE:]], axis=1)
            s = jnp.concatenate([s[:t - MAX_DISTANCE], bottom], axis=0)
        s_ref[slot][hh, a, cb] = s
        mt_ref[slot, hh, a, cb] = _col_max(s)

    def softmax_values(i, j, slot, hh, a, c):
        cb = c // w
        m_old = jnp.where(j == 0, -jnp.inf, m_ref[hh, a, cb])
        m_new = jnp.maximum(m_old, mt_ref[slot, hh, a, cb])
        alpha = jnp.exp2(m_old - m_new)
        p = jnp.exp2(s_ref[slot][hh, a, cb] - m_new).astype(_BF16)
        m_ref[hh, a, cb] = m_new
        vt = jnp.concatenate([vt_ref[0, j, head_cols(hh), :], jnp.ones((_SUM_ROWS, t), _BF16)], axis=0)
        acc_ref[i % 2, hh, a, cb] = alpha * acc_ref[i % 2, hh, a, cb] + _dot(vt, p)

    def finalize(i, hh, cb):
        lam = (jnp.exp(jnp.sum(lq1_ref[...] * lk1_ref[...])) - jnp.exp(jnp.sum(lq2_ref[...] * lk2_ref[...]))
               + lambda_init)
        a1 = acc_ref[i % 2, hh, 0, cb]
        a2 = acc_ref[i % 2, hh, 1, cb]
        r1 = 1.0 / a1[V_HEAD_DIM:V_HEAD_DIM + 1]
        r2 = lam / a2[V_HEAD_DIM:V_HEAD_DIM + 1]
        o = a1[:V_HEAD_DIM] * r1 - a2[:V_HEAD_DIM] * r2
        o = o * lax.rsqrt(jnp.mean(o * o, axis=0, keepdims=True) + EPS)
        o = o * (g_ref[...] * (1.0 - lambda_init))
        o_ref[_tile_rows(i, cb * w, w), head_cols(hh)] = o.T

    fin_pieces = [(hh, cb) for hh in range(heads) for cb in range(t // w)]

    def pipeline_step(sc, sv, fin):
        pending = list(fin_pieces) if fin is not None else []
        stride = max(len(parts) // max(len(fin_pieces), 1), 1)
        for n, (hh, a, c) in enumerate(parts):
            if sc is not None:
                scores(sc[0], sc[1], sc[3], sc[2], hh, a, c)
            if sv is not None:
                softmax_values(sv[0], sv[1], sv[3], hh, a, c)
            if pending and n % stride == stride - 1:
                finalize(fin, *pending.pop(0))
        while pending:
            finalize(fin, *pending.pop(0))

    def static_step(n):
        return int(steps[0, n]), int(steps[1, n]), int(steps[2, n]), n % 2

    def static_fin(n):
        return int(steps[0, n - 1]) if n >= 1 and steps[3, n] else None

    acc_ref[...] = jnp.zeros(acc_ref.shape, _F32)
    m_ref[...] = jnp.full(m_ref.shape, -jnp.inf, _F32)

    pipeline_step(static_step(0), None, None)

    full = nsteps - 1
    combos = sorted({(int(steps[2, n + 1]), int(steps[3, n])) for n in range(full)})

    def iteration(n, par):
        k_sc = tab_ref[2, n + 1]
        k_fin = tab_ref[3, n]
        for c_sc, c_fin in combos:
            @pl.when((k_sc == c_sc) & (k_fin == c_fin))
            def _():
                pipeline_step((tab_ref[0, n + 1], tab_ref[1, n + 1], c_sc, 1 - par),
                              (tab_ref[0, n], tab_ref[1, n], None, par),
                              tab_ref[0, jnp.maximum(n - 1, 0)] if c_fin else None)

    def body(pair, carry):
        iteration(2 * pair, 0)
        iteration(2 * pair + 1, 1)
        return carry

    lax.fori_loop(0, full // 2, body, 0)
    if full % 2:
        iteration(full - 1, (full - 1) % 2)

    pipeline_step(None, static_step(nsteps - 1), static_fin(nsteps - 1))
    pipeline_step(None, None, int(steps[0, nsteps - 1]))


def _attention(q, k, vt, bias, lq1, lk1, lq2, lk2, subln_g, *, batch, seq, lambda_init):
    n, d_qk = q.shape
    t = ATTN_TILE
    nq = seq // t
    nh = d_qk // V_HEAD_DIM
    hb = ATTN_HEADS
    aw = vt.shape[2]
    w, ncb = ATTN_COLS, t // ATTN_COLS
    steps = _attn_steps(nq)
    assert steps.shape[1] >= 3 and nh % hb == 0
    vec = _resident((1, HEAD_DIM))
    head_rows = pl.BlockSpec((seq, hb * V_HEAD_DIM), lambda b, h: (b, h))
    return pl.pallas_call(
        functools.partial(_attn_kernel, lambda_init=lambda_init, steps=steps, heads=hb),
        grid=(batch, nh // hb),
        in_specs=[pl.BlockSpec(memory_space=pltpu.SMEM), head_rows, head_rows,
                  pl.BlockSpec((1, nq, hb * V_HEAD_DIM, t), lambda b, h: (b, 0, h, 0)),
                  pl.BlockSpec((hb, t, w), lambda b, h: (h, 0, 0)),
                  vec, vec, vec, vec, _resident((V_HEAD_DIM, 1))],
        out_specs=head_rows,
        out_shape=jax.ShapeDtypeStruct((n, aw), _F32),
        scratch_shapes=[pltpu.VMEM((hb, 2, ncb, t, w), _F32), pltpu.VMEM((hb, 2, ncb, t, w), _F32),
                        pltpu.VMEM((2, hb, 2, ncb, 1, w), _F32), pltpu.VMEM((hb, 2, ncb, 1, w), _F32),
                        pltpu.VMEM((2, hb, 2, ncb, V_HEAD_DIM + _SUM_ROWS, w), _F32)],
        compiler_params=_params("parallel", "parallel"),
        name="diff_attn",
    )(jnp.asarray(steps), q, k, vt, bias, lq1, lk1, lq2, lk2, subln_g)


def _zero_after(value, zero_ref):
    bits = lax.bitcast_convert_type(value, jnp.uint32) & zero_ref[...]
    return lax.bitcast_convert_type(bits, _F32)


def _branch_mlp_kernel(x_ref, attn_ref, gate_ref, c_ref, halo_ref, wb_ref, zero_ref, lng_ref, lnb_ref,
                       wpw_ref, bpw_ref, wout_ref, g2_ref, wup_ref, wdown_ref, gf_ref, o_ref,
                       y_ref, pad_ref, sh_ref, *, final_norm, tiles_per_seq):
    s = pl.program_id(0)
    rows, d = x_ref.shape
    first_of_seq = s % tiles_per_seq == 0
    lead = CONV_HALO - (CONV_WIDTH - 1)
    span = CONV_SUB + CONV_HALO - SUBLANES
    groups = CONV_CHUNK // SUBLANES
    nchunks = rows // CONV_CHUNK
    dff = wup_ref.shape[1]
    ndots = 2 + 2 * (dff // d)

    @pl.when(s == 0)
    def _():
        y_ref[...] = jnp.zeros(y_ref.shape, _F32)

    lane_blocks = y_ref.shape[1] // CONV_LANES
    nunits = (rows // CONV_SUB) * lane_blocks

    def conv_unit(q, zero):
        sub, lb = divmod(q, lane_blocks)
        lanes = slice(lb * CONV_LANES, (lb + 1) * CONV_LANES)
        base = sub * CONV_SUB
        if sub == 0:
            halo = halo_ref[:, lanes]
            pad_ref[:CONV_HALO, lanes] = jnp.where(first_of_seq, jnp.zeros_like(halo), halo)
        else:
            pad_ref[:CONV_HALO, lanes] = c_ref[base - CONV_HALO:base, lanes]
        body = c_ref[base:base + CONV_SUB, lanes]
        if zero is not None:
            body = (body.reshape(CONV_SUB // SUBLANES, SUBLANES, -1) + zero[None]).reshape(body.shape)
        pad_ref[CONV_HALO:, lanes] = body
        for b in range(1, SUBLANES):
            sh_ref[b - 1, :span, lanes] = pad_ref[b:b + span, lanes]
        bias = wb_ref[CONV_WIDTH, :, lanes]
        if zero is not None:
            bias = bias + zero
        taps = [wb_ref[kt, :, lanes] for kt in range(CONV_WIDTH)]
        for r0 in range(0, CONV_SUB, CONV_CHUNK):
            acc = [bias] * groups
            for b in range(SUBLANES):
                cls = [(divmod(kt + lead, SUBLANES)[0], kt) for kt in range(CONV_WIDTH)
                       if (kt + lead) % SUBLANES == b]
                for rho in range(min(a for a, _ in cls), groups + max(a for a, _ in cls)):
                    r = r0 + SUBLANES * rho
                    xs = pad_ref[r:r + SUBLANES, lanes] if b == 0 else sh_ref[b - 1, r:r + SUBLANES, lanes]
                    for a, kt in cls:
                        if 0 <= rho - a < groups:
                            acc[rho - a] = acc[rho - a] + xs * taps[kt]
            for g in range(groups):
                r = base + r0 + SUBLANES * g
                y_ref[r:r + SUBLANES, lanes] = acc[g]

    anchors = {}
    for q in range(1, nunits):
        pos = (q - 1) * ndots / nunits
        dot_idx = int(pos)
        row = int((pos - dot_idx) * rows) // SUBLANES * SUBLANES
        anchors.setdefault(dot_idx, []).append((q, row))

    def after_dot(dot_idx, result):
        for q, row in anchors.get(dot_idx, []):
            lb = q % lane_blocks
            piece = result[row:row + SUBLANES, lb * CONV_LANES:(lb + 1) * CONV_LANES]
            conv_unit(q, _zero_after(piece, zero_ref))

    y = y_ref[...]
    mu = jnp.mean(y, axis=-1, keepdims=True)
    yc = y - mu
    var = jnp.mean(yc * yc, axis=-1, keepdims=True)
    z = yc * lax.rsqrt(var + EPS) * lng_ref[...] + lnb_ref[...]
    z = (z * jax.nn.sigmoid(z)).astype(_BF16)
    conv_unit(0, None)

    pw = _dot(z, wpw_ref[...])
    after_dot(0, pw)
    gates = gate_ref[...].astype(_F32)
    merged = gates[:, :d] * attn_ref[...] + gates[:, d:] * (pw + bpw_ref[...])
    out = _dot(merged.astype(_BF16), wout_ref[...])
    after_dot(1, out)
    h1 = x_ref[...] + out
    u = (_rms(h1) * g2_ref[...]).astype(_BF16)
    acc = jnp.zeros(h1.shape, _F32)
    for n, c0 in enumerate(range(0, dff, d)):
        up = _dot(u, wup_ref[:, c0:c0 + d])
        after_dot(2 + 2 * n, up)
        hid = jnp.maximum(up, 0.0)
        down = _dot((hid * hid).astype(_BF16), wdown_ref[c0:c0 + d, :])
        after_dot(3 + 2 * n, down)
        acc = acc + down
    h2 = h1 + acc
    if final_norm:
        h2 = _rms(h2) * gf_ref[...]
    o_ref[...] = h2


def _branch_mlp(x2, attn, gates, c, wb, ln_g, ln_b, wpw, bpw, wout, g2, wup, wdown, gf, *, seq, final_norm):
    n, d = x2.shape
    ch = c.shape[1]
    rows = MLP_ROWS
    nt = n // rows
    halo_per_tile = rows // CONV_HALO
    prev = lambda w: pl.BlockSpec((rows, w), lambda s: (jnp.maximum(s - 1, 0), 0))
    cur = lambda s: jnp.minimum(s, nt - 1)
    return pl.pallas_call(
        functools.partial(_branch_mlp_kernel, final_norm=final_norm, tiles_per_seq=seq // rows),
        grid=(nt + 1,),
        in_specs=[prev(d), prev(d), prev(2 * d),
                  pl.BlockSpec((rows, ch), lambda s: (cur(s), 0)),
                  pl.BlockSpec((CONV_HALO, ch), lambda s: (jnp.maximum(cur(s) * halo_per_tile - 1, 0), 0)),
                  _resident(wb.shape), _resident((SUBLANES, CONV_LANES)), _resident((1, ch)), _resident((1, ch)),
                  _resident(wpw.shape),
                  _resident((1, d)), _resident(wout.shape), _resident((1, d)), _resident(wup.shape),
                  _resident(wdown.shape), _resident((1, d))],
        out_specs=prev(d),
        out_shape=jax.ShapeDtypeStruct((n, d), _F32),
        scratch_shapes=[pltpu.VMEM((rows, ch), _F32),
                        pltpu.VMEM((CONV_SUB + CONV_HALO, ch), _F32),
                        pltpu.VMEM((SUBLANES - 1, CONV_SUB + CONV_HALO, ch), _F32)],
        compiler_params=_params("arbitrary"),
        name="branch_mlp",
    )(x2, attn, gates, c, c, wb, jnp.zeros((SUBLANES, CONV_LANES), jnp.uint32), ln_g, ln_b, wpw, bpw, wout, g2, wup,
      wdown, gf)


def kernel(x, rel_bias, final_norm_g, norm_mix_g, w_in, b_glu, b_gate, lam_q1, lam_k1, lam_q2, lam_k2,
           subln_g, conv_w, conv_b, conv_ln_g, conv_ln_b, w_pw2, b_pw2, w_out, norm_mlp_g, w_up, w_down):
    batch, seq, d = x.shape
    depth = w_in.shape[0]
    ch = conv_w.shape[2]
    d_qk = (w_in.shape[2] - 2 * ch - 2 * d) // 3
    c1, c2, c3 = 2 * d_qk, 3 * d_qk, 3 * d_qk + 2 * ch
    assert seq % ATTN_TILE == 0 and seq % PROJ_ROWS == 0 and PROJ_ROWS % ATTN_TILE == 0
    assert seq % MLP_ROWS == 0 and MLP_ROWS % CONV_SUB == 0 and ATTN_TILE >= MAX_DISTANCE

    row = lambda v: v.reshape(1, -1)
    h = x.reshape(batch * seq, d)
    bias = _bias_tiles(rel_bias)
    for l in range(depth):
        lambda_init = 0.8 - 0.6 * math.exp(-0.3 * l)
        w = w_in[l]
        q, k, vt, c, gates = _proj(
            h, row(norm_mix_g[l]), w[:, :c1].astype(_BF16), w[:, c1:c2].T.astype(_BF16),
            w[:, c2:c3].astype(_BF16), w[:, c3:].astype(_BF16), row(b_glu[l]), row(b_gate[l]),
            batch=batch, seq=seq)
        attn = _attention(q, k, vt, bias, row(lam_q1[l]), row(lam_k1[l]), row(lam_q2[l]), row(lam_k2[l]),
                          subln_g[l].reshape(-1, 1), batch=batch, seq=seq, lambda_init=lambda_init)
        wb = jnp.concatenate([conv_w[l], conv_b[l][None]], axis=0)
        wb = jnp.broadcast_to(wb[:, None, :], (CONV_WIDTH + 1, SUBLANES, ch))
        h = _branch_mlp(h, attn, gates, c, wb, row(conv_ln_g[l]), row(conv_ln_b[l]), w_pw2[l].astype(_BF16),
                        row(b_pw2[l]), w_out[l].astype(_BF16), row(norm_mlp_g[l]), w_up[l].astype(_BF16),
                        w_down[l].astype(_BF16), row(final_norm_g), seq=seq, final_norm=(l == depth - 1))
    return h.reshape(batch, seq, d)
```

```python
import functools
import math

import jax
import jax.numpy as jnp
import numpy as np
from jax import lax
from jax.experimental import pallas as pl
from jax.experimental.pallas import tpu as pltpu

HEAD_DIM = 64
V_HEAD_DIM = 2 * HEAD_DIM
CONV_WIDTH = 31
NUM_BUCKETS = 32
MAX_DISTANCE = 128
EPS = 1e-6
MASKED = -1e30
LOG2E = 1.4426950408889634

PROJ_ROWS = 512
ATTN_TILE = 512
ATTN_COLS = 256
ATTN_HEADS = 2
MLP_ROWS = 512
CONV_SUB = 128
CONV_HALO = 32
CONV_CHUNK = 64
CONV_LANES = 128
SUBLANES = 8
VMEM_LIMIT = 56 * 1024 * 1024

_BF16 = jnp.bfloat16
_F32 = jnp.float32


def _resident(shape):
    return pl.BlockSpec(shape, lambda *_: (0,) * len(shape), pipeline_mode=pl.Buffered(1))


def _params(*semantics):
    return pltpu.CompilerParams(dimension_semantics=semantics, vmem_limit_bytes=VMEM_LIMIT)


def _rms(xf):
    return xf * lax.rsqrt(jnp.mean(xf * xf, axis=-1, keepdims=True) + EPS)


def _dot(a, b):
    return jnp.dot(a, b, preferred_element_type=_F32)


def _dot_nt(a, b):
    return lax.dot_general(a, b, (((1,), (1,)), ((), ())), preferred_element_type=_F32)


def _proj_kernel(x_ref, g_ref, wqk_ref, wvt_ref, wglu_ref, wgate_ref, bglu_ref, bgate_ref,
                 q_ref, k_ref, vt_ref, c_ref, gate_ref, *, d_qk, kv_tiles_per_block):
    u = (_rms(x_ref[...]) * g_ref[...]).astype(_BF16)
    ch = c_ref.shape[-1]
    for g0 in range(0, gate_ref.shape[-1], ch):
        gate_ref[:, g0:g0 + ch] = jax.nn.sigmoid(
            _dot(u, wgate_ref[:, g0:g0 + ch]) + bgate_ref[:, g0:g0 + ch]).astype(_BF16)
    ga = _dot(u, wglu_ref[:, :ch]) + bglu_ref[:, :ch]
    gb = _dot(u, wglu_ref[:, ch:]) + bglu_ref[:, ch:]
    c_ref[...] = ga * jax.nn.sigmoid(gb)
    scale = LOG2E / math.sqrt(HEAD_DIM)
    q_ref[...] = (_dot(u, wqk_ref[:, :d_qk]) * scale).astype(_BF16)
    vt = _dot_nt(wvt_ref[...], u).astype(_BF16)
    for t in range(kv_tiles_per_block):
        vt_ref[0, t] = vt[:, t * ATTN_TILE:(t + 1) * ATTN_TILE]
    k_ref[...] = _dot(u, wqk_ref[:, d_qk:]).astype(_BF16)


def _proj(x2, g, wqk, wvt, wglu, wgate, bglu, bgate, *, batch, seq):
    n, d = x2.shape
    d_qk = wqk.shape[1] // 2
    aw = wvt.shape[0]
    ch = wglu.shape[1] // 2
    ngate = wgate.shape[1]
    rows = PROJ_ROWS
    tiles = rows // ATTN_TILE
    blocks_per_seq = seq // rows
    row_spec = lambda w: pl.BlockSpec((rows, w), lambda i: (i, 0))
    return pl.pallas_call(
        functools.partial(_proj_kernel, d_qk=d_qk, kv_tiles_per_block=tiles),
        grid=(n // rows,),
        in_specs=[row_spec(d), _resident((1, d)), _resident(wqk.shape), _resident(wvt.shape),
                  _resident(wglu.shape), _resident(wgate.shape), _resident(bglu.shape),
                  _resident(bgate.shape)],
        out_specs=[row_spec(d_qk), row_spec(d_qk),
                   pl.BlockSpec((1, tiles, aw, ATTN_TILE),
                                lambda i: (i // blocks_per_seq, i % blocks_per_seq, 0, 0)),
                   row_spec(ch), row_spec(ngate)],
        out_shape=[jax.ShapeDtypeStruct((n, d_qk), _BF16), jax.ShapeDtypeStruct((n, d_qk), _BF16),
                   jax.ShapeDtypeStruct((batch, seq // ATTN_TILE, aw, ATTN_TILE), _BF16),
                   jax.ShapeDtypeStruct((n, ch), _F32), jax.ShapeDtypeStruct((n, ngate), _BF16)],
        compiler_params=_params("parallel"),
        name="proj",
    )(x2, g, wqk, wvt, wglu, wgate, bglu, bgate)


def _bucket_table():
    t, w = ATTN_TILE, ATTN_COLS
    r = np.arange(t, dtype=np.int64)[:, None]
    c = np.arange(w, dtype=np.int64)[None, :]
    rel = (t - w) + c - r
    n = np.maximum(rel, 0)
    max_exact = NUM_BUCKETS // 2
    nf = np.maximum(n, 1).astype(np.float32)
    large = max_exact + (np.log(nf / np.float32(max_exact)) / np.float32(math.log(MAX_DISTANCE / max_exact))
                         * np.float32(NUM_BUCKETS - max_exact)).astype(np.int32)
    large = np.minimum(large, NUM_BUCKETS - 1)
    return np.where(rel >= 0, np.where(n < max_exact, n, large), -1).astype(np.int32)


def _bias_kernel(rb_ref, bucket_ref, out_ref):
    h = pl.program_id(0)
    bucket = bucket_ref[...]
    far = rb_ref[NUM_BUCKETS - 1, h]
    acc = jnp.full(bucket.shape, MASKED, _F32)
    for b in range(NUM_BUCKETS):
        acc = jnp.where(bucket == b, (rb_ref[b, h] - far) * LOG2E, acc)
    out_ref[0] = acc


def _bias_tiles(rel_bias):
    nh = rel_bias.shape[1]
    t, w = ATTN_TILE, ATTN_COLS
    return pl.pallas_call(
        _bias_kernel,
        grid=(nh,),
        in_specs=[pl.BlockSpec(memory_space=pltpu.SMEM), _resident((t, w))],
        out_specs=pl.BlockSpec((1, t, w), lambda h: (h, 0, 0)),
        out_shape=jax.ShapeDtypeStruct((nh, t, w), _F32),
        compiler_params=_params("parallel"),
        name="bias_tiles",
    )(rel_bias, jnp.asarray(_bucket_table()))


MAX_CHAINS = 8
_PLAIN, _DIAG = 0, 1
_SUM_ROWS = 16


def _attn_steps(nq):
    steps = [[i, j, _DIAG if j == i else _PLAIN, 0] for i in range(nq) for j in range(i + 1)]
    for n in range(1, len(steps)):
        steps[n][3] = int(steps[n - 1][2] == _DIAG)
    return np.asarray(steps, dtype=np.int32).T.copy()


def _col_max(s):
    groups = [s[r:r + SUBLANES] for r in range(0, s.shape[0], SUBLANES)]
    acc = groups[:MAX_CHAINS]
    for n, g in enumerate(groups[MAX_CHAINS:]):
        acc[n % MAX_CHAINS] = jnp.maximum(acc[n % MAX_CHAINS], g)
    while len(acc) > 1:
        acc = [jnp.maximum(acc[n], acc[n + 1]) for n in range(0, len(acc), 2)]
    return jnp.max(acc[0], axis=0, keepdims=True)


def _tile_rows(idx, offset=0, size=ATTN_TILE):
    start = idx * ATTN_TILE + offset
    if not isinstance(start, int):
        start = pl.multiple_of(start, math.gcd(ATTN_TILE, offset))
    return pl.ds(start, size)


def _attn_kernel(tab_ref, q_ref, k_ref, vt_ref, bias_ref, lq1_ref, lk1_ref, lq2_ref, lk2_ref, g_ref,
                 o_ref, s0_ref, s1_ref, mt_ref, m_ref, acc_ref, *, lambda_init, steps, heads):
    t, w = ATTN_TILE, ATTN_COLS
    s_ref = (s0_ref, s1_ref)
    nsteps = steps.shape[1]
    parts = [(hh, a, c) for hh in range(heads) for a in range(2) for c in range(0, t, w)]

    def head_cols(hh):
        return slice(hh * V_HEAD_DIM, (hh + 1) * V_HEAD_DIM)

    def scores(i, j, slot, kind, hh, a, c):
        cb = c // w
        q = q_ref[_tile_rows(i, c, w), head_cols(hh)]
        kk = k_ref[_tile_rows(j), head_cols(hh)]
        lane = lax.broadcasted_iota(jnp.int32, q.shape, 1)
        keep = lane < HEAD_DIM if a == 0 else lane >= HEAD_DIM
        s = _dot_nt(kk, jnp.where(keep, q, jnp.zeros_like(q)))
        if kind == _DIAG:
            bias = bias_ref[hh, t - w - c:, :]
            if c + w < t:
                bias = jnp.concatenate([bias, jnp.full((t - w - c, w), MASKED, _F32)], axis=0)
            s = s + bias
        elif c == 0:
            near = jnp.where(j == i - 1, 1.0, 0.0).astype(_F32)
            corner = s[t - MAX_DISTANCE:, :MAX_DISTANCE] + (
                bias_ref[hh, t - w - MAX_DISTANCE:t - w, :MAX_DISTANCE] * near)
            bottom = jnp.concatenate([corner, s[t - MAX_DISTANCE:, MAX_DISTANCE:]], axis=1)
            s = jnp.concatenate([s[:t - MAX_DISTANCE], bottom], axis=0)
        s_ref[slot][hh, a, cb] = s
        mt_ref[slot, hh, a, cb] = _col_max(s)

    def softmax_values(i, j, slot, hh, a, c):
        cb = c // w
        m_old = jnp.where(j == 0, -jnp.inf, m_ref[hh, a, cb])
        m_new = jnp.maximum(m_old, mt_ref[slot, hh, a, cb])
        alpha = jnp.exp2(m_old - m_new)
        p = jnp.exp2(s_ref[slot][hh, a, cb] - m_new).astype(_BF16)
        m_ref[hh, a, cb] = m_new
        vt = jnp.concatenate([vt_ref[0, j, head_cols(hh), :], jnp.ones((_SUM_ROWS, t), _BF16)], axis=0)
        acc_ref[i % 2, hh, a, cb] = alpha * acc_ref[i % 2, hh, a, cb] + _dot(vt, p)

    def finalize(i, hh, cb):
        lam = (jnp.exp(jnp.sum(lq1_ref[...] * lk1_ref[...])) - jnp.exp(jnp.sum(lq2_ref[...] * lk2_ref[...]))
               + lambda_init)
        a1 = acc_ref[i % 2, hh, 0, cb]
        a2 = acc_ref[i % 2, hh, 1, cb]
        r1 = 1.0 / a1[V_HEAD_DIM:V_HEAD_DIM + 1]
        r2 = lam / a2[V_HEAD_DIM:V_HEAD_DIM + 1]
        o = a1[:V_HEAD_DIM] * r1 - a2[:V_HEAD_DIM] * r2
        o = o * lax.rsqrt(jnp.mean(o * o, axis=0, keepdims=True) + EPS)
        o = o * (g_ref[...] * (1.0 - lambda_init))
        o_ref[_tile_rows(i, cb * w, w), head_cols(hh)] = o.T

    fin_pieces = [(hh, cb) for hh in range(heads) for cb in range(t // w)]

    def pipeline_step(sc, sv, fin):
        pending = list(fin_pieces) if fin is not None else []
        stride = max(len(parts) // max(len(fin_pieces), 1), 1)
        for n, (hh, a, c) in enumerate(parts):
            if sc is not None:
                scores(sc[0], sc[1], sc[3], sc[2], hh, a, c)
            if sv is not None:
                softmax_values(sv[0], sv[1], sv[3], hh, a, c)
            if pending and n % stride == stride - 1:
                finalize(fin, *pending.pop(0))
        while pending:
            finalize(fin, *pending.pop(0))

    def static_step(n):
        return int(steps[0, n]), int(steps[1, n]), int(steps[2, n]), n % 2

    def static_fin(n):
        return int(steps[0, n - 1]) if n >= 1 and steps[3, n] else None

    acc_ref[...] = jnp.zeros(acc_ref.shape, _F32)
    m_ref[...] = jnp.full(m_ref.shape, -jnp.inf, _F32)

    pipeline_step(static_step(0), None, None)

    full = nsteps - 1
    combos = sorted({(int(steps[2, n + 1]), int(steps[3, n])) for n in range(full)})

    def iteration(n, par):
        k_sc = tab_ref[2, n + 1]
        k_fin = tab_ref[3, n]
        for c_sc, c_fin in combos:
            @pl.when((k_sc == c_sc) & (k_fin == c_fin))
            def _():
                pipeline_step((tab_ref[0, n + 1], tab_ref[1, n + 1], c_sc, 1 - par),
                              (tab_ref[0, n], tab_ref[1, n], None, par),
                              tab_ref[0, jnp.maximum(n - 1, 0)] if c_fin else None)

    def body(pair, carry):
        iteration(2 * pair, 0)
        iteration(2 * pair + 1, 1)
        return carry

    lax.fori_loop(0, full // 2, body, 0)
    if full % 2:
        iteration(full - 1, (full - 1) % 2)

    pipeline_step(None, static_step(nsteps - 1), static_fin(nsteps - 1))
    pipeline_step(None, None, int(steps[0, nsteps - 1]))


def _attention(q, k, vt, bias, lq1, lk1, lq2, lk2, subln_g, *, batch, seq, lambda_init):
    n, d_qk = q.shape
    t = ATTN_TILE
    nq = seq // t
    nh = d_qk // V_HEAD_DIM
    hb = ATTN_HEADS
    aw = vt.shape[2]
    w, ncb = ATTN_COLS, t // ATTN_COLS
    steps = _attn_steps(nq)
    assert steps.shape[1] >= 3 and nh % hb == 0
    vec = _resident((1, HEAD_DIM))
    head_rows = pl.BlockSpec((seq, hb * V_HEAD_DIM), lambda b, h: (b, h))
    return pl.pallas_call(
        functools.partial(_attn_kernel, lambda_init=lambda_init, steps=steps, heads=hb),
        grid=(batch, nh // hb),
        in_specs=[pl.BlockSpec(memory_space=pltpu.SMEM), head_rows, head_rows,
                  pl.BlockSpec((1, nq, hb * V_HEAD_DIM, t), lambda b, h: (b, 0, h, 0)),
                  pl.BlockSpec((hb, t, w), lambda b, h: (h, 0, 0)),
                  vec, vec, vec, vec, _resident((V_HEAD_DIM, 1))],
        out_specs=head_rows,
        out_shape=jax.ShapeDtypeStruct((n, aw), _F32),
        scratch_shapes=[pltpu.VMEM((hb, 2, ncb, t, w), _F32), pltpu.VMEM((hb, 2, ncb, t, w), _F32),
                        pltpu.VMEM((2, hb, 2, ncb, 1, w), _F32), pltpu.VMEM((hb, 2, ncb, 1, w), _F32),
                        pltpu.VMEM((2, hb, 2, ncb, V_HEAD_DIM + _SUM_ROWS, w), _F32)],
        compiler_params=_params("parallel", "parallel"),
        name="diff_attn",
    )(jnp.asarray(steps), q, k, vt, bias, lq1, lk1, lq2, lk2, subln_g)


def _zero_after(value, zero_ref):
    bits = lax.bitcast_convert_type(value, jnp.uint32) & zero_ref[...]
    return lax.bitcast_convert_type(bits, _F32)


def _branch_mlp_kernel(x_ref, attn_ref, gate_ref, c_ref, halo_ref, wb_ref, zero_ref, lng_ref, lnb_ref,
                       wpw_ref, bpw_ref, wout_ref, g2_ref, wup_ref, wdown_ref, gf_ref, o_ref,
                       y_ref, pad_ref, sh_ref, *, final_norm, tiles_per_seq):
    s = pl.program_id(0)
    rows, d = x_ref.shape
    first_of_seq = s % tiles_per_seq == 0
    lead = CONV_HALO - (CONV_WIDTH - 1)
    span = CONV_SUB + CONV_HALO - SUBLANES
    groups = CONV_CHUNK // SUBLANES
    dff = wup_ref.shape[1]
    ndots = 2 + 2 * (dff // d)

    @pl.when(s == 0)
    def _():
        y_ref[...] = jnp.zeros(y_ref.shape, _F32)

    lane_blocks = y_ref.shape[1] // CONV_LANES
    nunits = (rows // CONV_SUB) * lane_blocks

    def conv_unit(q, zero):
        sub, lb = divmod(q, lane_blocks)
        lanes = slice(lb * CONV_LANES, (lb + 1) * CONV_LANES)
        base = sub * CONV_SUB
        if sub == 0:
            halo = halo_ref[:, lanes]
            pad_ref[:CONV_HALO, lanes] = jnp.where(first_of_seq, jnp.zeros_like(halo), halo)
        else:
            pad_ref[:CONV_HALO, lanes] = c_ref[base - CONV_HALO:base, lanes]
        body = c_ref[base:base + CONV_SUB, lanes]
        if zero is not None:
            body = (body.reshape(CONV_SUB // SUBLANES, SUBLANES, -1) + zero[None]).reshape(body.shape)
        pad_ref[CONV_HALO:, lanes] = body
        for b in range(1, SUBLANES):
            sh_ref[b - 1, :span, lanes] = pad_ref[b:b + span, lanes]
        bias = wb_ref[CONV_WIDTH, :, lanes]
        if zero is not None:
            bias = bias + zero
        taps = [wb_ref[kt, :, lanes] for kt in range(CONV_WIDTH)]
        for r0 in range(0, CONV_SUB, CONV_CHUNK):
            acc = [bias] * groups
            for b in range(SUBLANES):
                cls = [(divmod(kt + lead, SUBLANES)[0], kt) for kt in range(CONV_WIDTH)
                       if (kt + lead) % SUBLANES == b]
                for rho in range(min(a for a, _ in cls), groups + max(a for a, _ in cls)):
                    r = r0 + SUBLANES * rho
                    xs = pad_ref[r:r + SUBLANES, lanes] if b == 0 else sh_ref[b - 1, r:r + SUBLANES, lanes]
                    for a, kt in cls:
                        if 0 <= rho - a < groups:
                            acc[rho - a] = acc[rho - a] + xs * taps[kt]
            for g in range(groups):
                r = base + r0 + SUBLANES * g
                y_ref[r:r + SUBLANES, lanes] = acc[g]

    anchors = {}
    for q in range(1, nunits):
        pos = (q - 1) * ndots / nunits
        dot_idx = int(pos)
        row = int((pos - dot_idx) * rows) // SUBLANES * SUBLANES
        anchors.setdefault(dot_idx, []).append((q, row))

    def after_dot(dot_idx, result):
        for q, row in anchors.get(dot_idx, []):
            lb = q % lane_blocks
            piece = result[row:row + SUBLANES, lb * CONV_LANES:(lb + 1) * CONV_LANES]
            conv_unit(q, _zero_after(piece, zero_ref))

    y = y_ref[...]
    mu = jnp.mean(y, axis=-1, keepdims=True)
    yc = y - mu
    var = jnp.mean(yc * yc, axis=-1, keepdims=True)
    z = yc * lax.rsqrt(var + EPS) * lng_ref[...] + lnb_ref[...]
    z = (z * jax.nn.sigmoid(z)).astype(_BF16)
    conv_unit(0, None)

    pw = _dot(z, wpw_ref[...])
    after_dot(0, pw)
    gates = gate_ref[...].astype(_F32)
    merged = gates[:, :d] * attn_ref[...] + gates[:, d:] * (pw + bpw_ref[...])
    out = _dot(merged.astype(_BF16), wout_ref[...])
    after_dot(1, out)
    h1 = x_ref[...] + out
    u = (_rms(h1) * g2_ref[...]).astype(_BF16)
    acc = jnp.zeros(h1.shape, _F32)
    for n, c0 in enumerate(range(0, dff, d)):
        up = _dot(u, wup_ref[:, c0:c0 + d])
        after_dot(2 + 2 * n, up)
        hid = jnp.maximum(up, 0.0)
        down = _dot((hid * hid).astype(_BF16), wdown_ref[c0:c0 + d, :])
        after_dot(3 + 2 * n, down)
        acc = acc + down
    h2 = h1 + acc
    if final_norm:
        h2 = _rms(h2) * gf_ref[...]
    o_ref[...] = h2


def _branch_mlp(x2, attn, gates, c, wb, ln_g, ln_b, wpw, bpw, wout, g2, wup, wdown, gf, *, seq, final_norm):
    n, d = x2.shape
    ch = c.shape[1]
    rows = MLP_ROWS
    nt = n // rows
    halo_per_tile = rows // CONV_HALO
    prev = lambda w: pl.BlockSpec((rows, w), lambda s: (jnp.maximum(s - 1, 0), 0))
    cur = lambda s: jnp.minimum(s, nt - 1)
    return pl.pallas_call(
        functools.partial(_branch_mlp_kernel, final_norm=final_norm, tiles_per_seq=seq // rows),
        grid=(nt + 1,),
        in_specs=[prev(d), prev(d), prev(2 * d),
                  pl.BlockSpec((rows, ch), lambda s: (cur(s), 0)),
                  pl.BlockSpec((CONV_HALO, ch), lambda s: (jnp.maximum(cur(s) * halo_per_tile - 1, 0), 0)),
                  _resident(wb.shape), _resident((SUBLANES, CONV_LANES)), _resident((1, ch)), _resident((1, ch)),
                  _resident(wpw.shape),
                  _resident((1, d)), _resident(wout.shape), _resident((1, d)), _resident(wup.shape),
                  _resident(wdown.shape), _resident((1, d))],
        out_specs=prev(d),
        out_shape=jax.ShapeDtypeStruct((n, d), _F32),
        scratch_shapes=[pltpu.VMEM((rows, ch), _F32),
                        pltpu.VMEM((CONV_SUB + CONV_HALO, ch), _F32),
                        pltpu.VMEM((SUBLANES - 1, CONV_SUB + CONV_HALO, ch), _F32)],
        compiler_params=_params("arbitrary"),
        name="branch_mlp",
    )(x2, attn, gates, c, c, wb, jnp.zeros((SUBLANES, CONV_LANES), jnp.uint32), ln_g, ln_b, wpw, bpw, wout, g2, wup,
      wdown, gf)


def kernel(x, rel_bias, final_norm_g, norm_mix_g, w_in, b_glu, b_gate, lam_q1, lam_k1, lam_q2, lam_k2,
           subln_g, conv_w, conv_b, conv_ln_g, conv_ln_b, w_pw2, b_pw2, w_out, norm_mlp_g, w_up, w_down):
    batch, seq, d = x.shape
    depth = w_in.shape[0]
    ch = conv_w.shape[2]
    d_qk = (w_in.shape[2] - 2 * ch - 2 * d) // 3
    c1, c2, c3 = 2 * d_qk, 3 * d_qk, 3 * d_qk + 2 * ch
    assert seq % ATTN_TILE == 0 and seq % PROJ_ROWS == 0 and PROJ_ROWS % ATTN_TILE == 0
    assert seq % MLP_ROWS == 0 and MLP_ROWS % CONV_SUB == 0 and ATTN_TILE >= MAX_DISTANCE

    row = lambda v: v.reshape(1, -1)
    h = x.reshape(batch * seq, d)
    bias = _bias_tiles(rel_bias)
    for l in range(depth):
        lambda_init = 0.8 - 0.6 * math.exp(-0.3 * l)
        w = w_in[l]
        q, k, vt, c, gates = _proj(
            h, row(norm_mix_g[l]), w[:, :c1].astype(_BF16), w[:, c1:c2].T.astype(_BF16),
            w[:, c2:c3].astype(_BF16), w[:, c3:].astype(_BF16), row(b_glu[l]), row(b_gate[l]),
            batch=batch, seq=seq)
        attn = _attention(q, k, vt, bias, row(lam_q1[l]), row(lam_k1[l]), row(lam_q2[l]), row(lam_k2[l]),
                          subln_g[l].reshape(-1, 1), batch=batch, seq=seq, lambda_init=lambda_init)
        wb = jnp.concatenate([conv_w[l], conv_b[l][None]], axis=0)
        wb = jnp.broadcast_to(wb[:, None, :], (CONV_WIDTH + 1, SUBLANES, ch))
        h = _branch_mlp(h, attn, gates, c, wb, row(conv_ln_g[l]), row(conv_ln_b[l]), w_pw2[l].astype(_BF16),
                        row(b_pw2[l]), w_out[l].astype(_BF16), row(norm_mlp_g[l]), w_up[l].astype(_BF16),
                        w_down[l].astype(_BF16), row(final_norm_g), seq=seq, final_norm=(l == depth - 1))
    return h.reshape(batch, seq, d)
```

```python
import functools
import math

import jax
import jax.numpy as jnp
import numpy as np
from jax import lax
from jax.experimental import pallas as pl
from jax.experimental.pallas import tpu as pltpu

HEAD_DIM = 64
V_HEAD_DIM = 2 * HEAD_DIM
CONV_WIDTH = 31
NUM_BUCKETS = 32
MAX_DISTANCE = 128
EPS = 1e-6
MASKED = -1e30
LOG2E = 1.4426950408889634

PROJ_ROWS = 512
ATTN_TILE = 512
ATTN_COLS = 256
ATTN_HEADS = 2
MLP_ROWS = 512
CONV_SUB = 128
CONV_HALO = 32
CONV_CHUNK = 64
CONV_LANES = 128
SUBLANES = 8
VMEM_LIMIT = 56 * 1024 * 1024

_BF16 = jnp.bfloat16
_F32 = jnp.float32


def _resident(shape):
    return pl.BlockSpec(shape, lambda *_: (0,) * len(shape), pipeline_mode=pl.Buffered(1))


def _params(*semantics):
    return pltpu.CompilerParams(dimension_semantics=semantics, vmem_limit_bytes=VMEM_LIMIT)


def _rms(xf):
    return xf * lax.rsqrt(jnp.mean(xf * xf, axis=-1, keepdims=True) + EPS)


def _dot(a, b):
    return jnp.dot(a, b, preferred_element_type=_F32)


def _dot_nt(a, b):
    return lax.dot_general(a, b, (((1,), (1,)), ((), ())), preferred_element_type=_F32)


def _proj_kernel(x_ref, g_ref, wqk_ref, wvt_ref, wglu_ref, wgate_ref, bglu_ref, bgate_ref,
                 q_ref, k_ref, vt_ref, c_ref, gate_ref, *, d_qk, kv_tiles_per_block):
    u = (_rms(x_ref[...]) * g_ref[...]).astype(_BF16)
    ch = c_ref.shape[-1]
    for g0 in range(0, gate_ref.shape[-1], ch):
        gate_ref[:, g0:g0 + ch] = jax.nn.sigmoid(
            _dot(u, wgate_ref[:, g0:g0 + ch]) + bgate_ref[:, g0:g0 + ch]).astype(_BF16)
    ga = _dot(u, wglu_ref[:, :ch]) + bglu_ref[:, :ch]
    gb = _dot(u, wglu_ref[:, ch:]) + bglu_ref[:, ch:]
    c_ref[...] = ga * jax.nn.sigmoid(gb)
    scale = LOG2E / math.sqrt(HEAD_DIM)
    q_ref[...] = (_dot(u, wqk_ref[:, :d_qk]) * scale).astype(_BF16)
    vt = _dot_nt(wvt_ref[...], u).astype(_BF16)
    for t in range(kv_tiles_per_block):
        vt_ref[0, t] = vt[:, t * ATTN_TILE:(t + 1) * ATTN_TILE]
    k_ref[...] = _dot(u, wqk_ref[:, d_qk:]).astype(_BF16)


def _proj(x2, g, wqk, wvt, wglu, wgate, bglu, bgate, *, batch, seq):
    n, d = x2.shape
    d_qk = wqk.shape[1] // 2
    aw = wvt.shape[0]
    ch = wglu.shape[1] // 2
    ngate = wgate.shape[1]
    rows = PROJ_ROWS
    tiles = rows // ATTN_TILE
    blocks_per_seq = seq // rows
    row_spec = lambda w: pl.BlockSpec((rows, w), lambda i: (i, 0))
    return pl.pallas_call(
        functools.partial(_proj_kernel, d_qk=d_qk, kv_tiles_per_block=tiles),
        grid=(n // rows,),
        in_specs=[row_spec(d), _resident((1, d)), _resident(wqk.shape), _resident(wvt.shape),
                  _resident(wglu.shape), _resident(wgate.shape), _resident(bglu.shape),
                  _resident(bgate.shape)],
        out_specs=[row_spec(d_qk), row_spec(d_qk),
                   pl.BlockSpec((1, tiles, aw, ATTN_TILE),
                                lambda i: (i // blocks_per_seq, i % blocks_per_seq, 0, 0)),
                   row_spec(ch), row_spec(ngate)],
        out_shape=[jax.ShapeDtypeStruct((n, d_qk), _BF16), jax.ShapeDtypeStruct((n, d_qk), _BF16),
                   jax.ShapeDtypeStruct((batch, seq // ATTN_TILE, aw, ATTN_TILE), _BF16),
                   jax.ShapeDtypeStruct((n, ch), _F32), jax.ShapeDtypeStruct((n, ngate), _BF16)],
        compiler_params=_params("parallel"),
        name="proj",
    )(x2, g, wqk, wvt, wglu, wgate, bglu, bgate)


def _bucket_table():
    t, w = ATTN_TILE, ATTN_COLS
    r = np.arange(t, dtype=np.int64)[:, None]
    c = np.arange(w, dtype=np.int64)[None, :]
    rel = (t - w) + c - r
    n = np.maximum(rel, 0)
    max_exact = NUM_BUCKETS // 2
    nf = np.maximum(n, 1).astype(np.float32)
    large = max_exact + (np.log(nf / np.float32(max_exact)) / np.float32(math.log(MAX_DISTANCE / max_exact))
                         * np.float32(NUM_BUCKETS - max_exact)).astype(np.int32)
    large = np.minimum(large, NUM_BUCKETS - 1)
    return np.where(rel >= 0, np.where(n < max_exact, n, large), -1).astype(np.int32)


def _bias_kernel(rb_ref, bucket_ref, out_ref):
    h = pl.program_id(0)
    bucket = bucket_ref[...]
    far = rb_ref[NUM_BUCKETS - 1, h]
    acc = jnp.full(bucket.shape, MASKED, _F32)
    for b in range(NUM_BUCKETS):
        acc = jnp.where(bucket == b, (rb_ref[b, h] - far) * LOG2E, acc)
    out_ref[0] = acc


def _bias_tiles(rel_bias):
    nh = rel_bias.shape[1]
    t, w = ATTN_TILE, ATTN_COLS
    return pl.pallas_call(
        _bias_kernel,
        grid=(nh,),
        in_specs=[pl.BlockSpec(memory_space=pltpu.SMEM), _resident((t, w))],
        out_specs=pl.BlockSpec((1, t, w), lambda h: (h, 0, 0)),
        out_shape=jax.ShapeDtypeStruct((nh, t, w), _F32),
        compiler_params=_params("parallel"),
        name="bias_tiles",
    )(rel_bias, jnp.asarray(_bucket_table()))


MAX_CHAINS = 8
_PLAIN, _DIAG = 0, 1
_SUM_ROWS = 16


def _attn_steps(nq):
    steps = [[i, j, _DIAG if j == i else _PLAIN, 0] for i in range(nq) for j in range(i + 1)]
    for n in range(1, len(steps)):
        steps[n][3] = int(steps[n - 1][2] == _DIAG)
    return np.asarray(steps, dtype=np.int32).T.copy()


def _col_max(s):
    groups = [s[r:r + SUBLANES] for r in range(0, s.shape[0], SUBLANES)]
    acc = groups[:MAX_CHAINS]
    for n, g in enumerate(groups[MAX_CHAINS:]):
        acc[n % MAX_CHAINS] = jnp.maximum(acc[n % MAX_CHAINS], g)
    while len(acc) > 1:
        acc = [jnp.maximum(acc[n], acc[n + 1]) for n in range(0, len(acc), 2)]
    return jnp.max(acc[0], axis=0, keepdims=True)


def _tile_rows(idx, offset=0, size=ATTN_TILE):
    start = idx * ATTN_TILE + offset
    if not isinstance(start, int):
        start = pl.multiple_of(start, math.gcd(ATTN_TILE, offset))
    return pl.ds(start, size)


def _attn_kernel(tab_ref, q_ref, k_ref, vt_ref, bias_ref, lq1_ref, lk1_ref, lq2_ref, lk2_ref, g_ref,
                 o_ref, s0_ref, s1_ref, mt_ref, m_ref, acc_ref, *, lambda_init, steps, heads):
    t, w = ATTN_TILE, ATTN_COLS
    s_ref = (s0_ref, s1_ref)
    nsteps = steps.shape[1]
    parts = [(hh, a, c) for hh in range(heads) for a in range(2) for c in range(0, t, w)]

    def head_cols(hh):
        return slice(hh * V_HEAD_DIM, (hh + 1) * V_HEAD_DIM)

    def scores(i, j, slot, kind, hh, a, c):
        cb = c // w
        q = q_ref[_tile_rows(i, c, w), head_cols(hh)]
        kk = k_ref[_tile_rows(j), head_cols(hh)]
        lane = lax.broadcasted_iota(jnp.int32, q.shape, 1)
        keep = lane < HEAD_DIM if a == 0 else lane >= HEAD_DIM
        s = _dot_nt(kk, jnp.where(keep, q, jnp.zeros_like(q)))
        if kind == _DIAG:
            bias = bias_ref[hh, t - w - c:, :]
            if c + w < t:
                bias = jnp.concatenate([bias, jnp.full((t - w - c, w), MASKED, _F32)], axis=0)
            s = s + bias
        elif c == 0:
            near = jnp.where(j == i - 1, 1.0, 0.0).astype(_F32)
            corner = s[t - MAX_DISTANCE:, :MAX_DISTANCE] + (
                bias_ref[hh, t - w - MAX_DISTANCE:t - w, :MAX_DISTANCE] * near)
            bottom = jnp.concatenate([corner, s[t - MAX_DISTANCE:, MAX_DISTANCE:]], axis=1)
            s = jnp.concatenate([s[:t - MAX_DISTANCE], bottom], axis=0)
        s_ref[slot][hh, a, cb] = s
        mt_ref[slot, hh, a, cb] = _col_max(s)

    def softmax_values(i, j, slot, hh, a, c):
        cb = c // w
        m_old = jnp.where(j == 0, -jnp.inf, m_ref[hh, a, cb])
        m_new = jnp.maximum(m_old, mt_ref[slot, hh, a, cb])
        alpha = jnp.exp2(m_old - m_new)
        p = jnp.exp2(s_ref[slot][hh, a, cb] - m_new).astype(_BF16)
        m_ref[hh, a, cb] = m_new
        vt = jnp.concatenate([vt_ref[0, j, head_cols(hh), :], jnp.ones((_SUM_ROWS, t), _BF16)], axis=0)
        acc_ref[i % 2, hh, a, cb] = alpha * acc_ref[i % 2, hh, a, cb] + _dot(vt, p)

    def finalize(i, hh, cb):
        lam = (jnp.exp(jnp.sum(lq1_ref[...] * lk1_ref[...])) - jnp.exp(jnp.sum(lq2_ref[...] * lk2_ref[...]))
               + lambda_init)
        a1 = acc_ref[i % 2, hh, 0, cb]
        a2 = acc_ref[i % 2, hh, 1, cb]
        r1 = 1.0 / a1[V_HEAD_DIM:V_HEAD_DIM + 1]
        r2 = lam / a2[V_HEAD_DIM:V_HEAD_DIM + 1]
        o = a1[:V_HEAD_DIM] * r1 - a2[:V_HEAD_DIM] * r2
        o = o * lax.rsqrt(jnp.mean(o * o, axis=0, keepdims=True) + EPS)
        o = o * (g_ref[...] * (1.0 - lambda_init))
        o_ref[_tile_rows(i, cb * w, w), head_cols(hh)] = o.T

    fin_pieces = [(hh, cb) for hh in range(heads) for cb in range(t // w)]

    def pipeline_step(sc, sv, fin):
        for hh, a, c in parts:
            if sc is not None:
                scores(sc[0], sc[1], sc[3], sc[2], hh, a, c)
            if sv is not None:
                softmax_values(sv[0], sv[1], sv[3], hh, a, c)
        if fin is not None:
            for hh, cb in fin_pieces:
                finalize(fin, hh, cb)

    def static_step(n):
        return int(steps[0, n]), int(steps[1, n]), int(steps[2, n]), n % 2

    def static_fin(n):
        return int(steps[0, n - 1]) if n >= 1 and steps[3, n] else None

    acc_ref[...] = jnp.zeros(acc_ref.shape, _F32)
    m_ref[...] = jnp.full(m_ref.shape, -jnp.inf, _F32)

    pipeline_step(static_step(0), None, None)

    full = nsteps - 1
    combos = sorted({(int(steps[2, n + 1]), int(steps[3, n])) for n in range(full)})

    def iteration(n, par):
        k_sc = tab_ref[2, n + 1]
        k_fin = tab_ref[3, n]
        for c_sc, c_fin in combos:
            @pl.when((k_sc == c_sc) & (k_fin == c_fin))
            def _():
                pipeline_step((tab_ref[0, n + 1], tab_ref[1, n + 1], c_sc, 1 - par),
                              (tab_ref[0, n], tab_ref[1, n], None, par),
                              tab_ref[0, jnp.maximum(n - 1, 0)] if c_fin else None)

    def body(pair, carry):
        iteration(2 * pair, 0)
        iteration(2 * pair + 1, 1)
        return carry

    lax.fori_loop(0, full // 2, body, 0)
    if full % 2:
        iteration(full - 1, (full - 1) % 2)

    pipeline_step(None, static_step(nsteps - 1), static_fin(nsteps - 1))
    pipeline_step(None, None, int(steps[0, nsteps - 1]))


def _attention(q, k, vt, bias, lq1, lk1, lq2, lk2, subln_g, *, batch, seq, lambda_init):
    n, d_qk = q.shape
    t = ATTN_TILE
    nq = seq // t
    nh = d_qk // V_HEAD_DIM
    hb = ATTN_HEADS
    aw = vt.shape[2]
    w, ncb = ATTN_COLS, t // ATTN_COLS
    steps = _attn_steps(nq)
    assert steps.shape[1] >= 3 and nh % hb == 0
    vec = _resident((1, HEAD_DIM))
    head_rows = pl.BlockSpec((seq, hb * V_HEAD_DIM), lambda b, h: (b, h))
    return pl.pallas_call(
        functools.partial(_attn_kernel, lambda_init=lambda_init, steps=steps, heads=hb),
        grid=(batch, nh // hb),
        in_specs=[pl.BlockSpec(memory_space=pltpu.SMEM), head_rows, head_rows,
                  pl.BlockSpec((1, nq, hb * V_HEAD_DIM, t), lambda b, h: (b, 0, h, 0)),
                  pl.BlockSpec((hb, t, w), lambda b, h: (h, 0, 0)),
                  vec, vec, vec, vec, _resident((V_HEAD_DIM, 1))],
        out_specs=head_rows,
        out_shape=jax.ShapeDtypeStruct((n, aw), _F32),
        scratch_shapes=[pltpu.VMEM((hb, 2, ncb, t, w), _F32), pltpu.VMEM((hb, 2, ncb, t, w), _F32),
                        pltpu.VMEM((2, hb, 2, ncb, 1, w), _F32), pltpu.VMEM((hb, 2, ncb, 1, w), _F32),
                        pltpu.VMEM((2, hb, 2, ncb, V_HEAD_DIM + _SUM_ROWS, w), _F32)],
        compiler_params=_params("parallel", "parallel"),
        name="diff_attn",
    )(jnp.asarray(steps), q, k, vt, bias, lq1, lk1, lq2, lk2, subln_g)


def _zero_after(value, zero_ref):
    bits = lax.bitcast_convert_type(value, jnp.uint32) & zero_ref[...]
    return lax.bitcast_convert_type(bits, _F32)


def _branch_mlp_kernel(x_ref, attn_ref, gate_ref, c_ref, halo_ref, wb_ref, zero_ref, lng_ref, lnb_ref,
                       wpw_ref, bpw_ref, wout_ref, g2_ref, wup_ref, wdown_ref, gf_ref, o_ref,
                       y_ref, pad_ref, sh_ref, *, final_norm, tiles_per_seq):
    s = pl.program_id(0)
    rows, d = x_ref.shape
    first_of_seq = s % tiles_per_seq == 0
    lead = CONV_HALO - (CONV_WIDTH - 1)
    span = CONV_SUB + CONV_HALO - SUBLANES
    groups = CONV_CHUNK // SUBLANES
    dff = wup_ref.shape[1]
    ndots = 2 + 2 * (dff // d)

    @pl.when(s == 0)
    def _():
        y_ref[...] = jnp.zeros(y_ref.shape, _F32)

    lane_blocks = y_ref.shape[1] // CONV_LANES
    nunits = (rows // CONV_SUB) * lane_blocks

    def conv_unit(q, zero):
        sub, lb = divmod(q, lane_blocks)
        lanes = slice(lb * CONV_LANES, (lb + 1) * CONV_LANES)
        base = sub * CONV_SUB
        if sub == 0:
            halo = halo_ref[:, lanes]
            pad_ref[:CONV_HALO, lanes] = jnp.where(first_of_seq, jnp.zeros_like(halo), halo)
        else:
            pad_ref[:CONV_HALO, lanes] = c_ref[base - CONV_HALO:base, lanes]
        body = c_ref[base:base + CONV_SUB, lanes]
        if zero is not None:
            body = (body.reshape(CONV_SUB // SUBLANES, SUBLANES, -1) + zero[None]).reshape(body.shape)
        pad_ref[CONV_HALO:, lanes] = body
        for b in range(1, SUBLANES):
            sh_ref[b - 1, :span, lanes] = pad_ref[b:b + span, lanes]
        bias = wb_ref[CONV_WIDTH, :, lanes]
        if zero is not None:
            bias = bias + zero
        taps = [wb_ref[kt, :, lanes] for kt in range(CONV_WIDTH)]
        for r0 in range(0, CONV_SUB, CONV_CHUNK):
            acc = [bias] * groups
            for b in range(SUBLANES):
                cls = [(divmod(kt + lead, SUBLANES)[0], kt) for kt in range(CONV_WIDTH)
                       if (kt + lead) % SUBLANES == b]
                for rho in range(min(a for a, _ in cls), groups + max(a for a, _ in cls)):
                    r = r0 + SUBLANES * rho
                    xs = pad_ref[r:r + SUBLANES, lanes] if b == 0 else sh_ref[b - 1, r:r + SUBLANES, lanes]
                    for a, kt in cls:
                        if 0 <= rho - a < groups:
                            acc[rho - a] = acc[rho - a] + xs * taps[kt]
            for g in range(groups):
                r = base + r0 + SUBLANES * g
                y_ref[r:r + SUBLANES, lanes] = acc[g]

    anchors = {}
    for q in range(1, nunits):
        pos = q * ndots / nunits
        dot_idx = int(pos)
        row = int((pos - dot_idx) * rows) // SUBLANES * SUBLANES
        anchors.setdefault(dot_idx, []).append((q, row))

    def after_dot(dot_idx, result):
        for q, row in anchors.get(dot_idx, []):
            lb = q % lane_blocks
            piece = result[row:row + SUBLANES, lb * CONV_LANES:(lb + 1) * CONV_LANES]
            conv_unit(q, _zero_after(piece, zero_ref))

    y = y_ref[...]
    mu = jnp.mean(y, axis=-1, keepdims=True)
    yc = y - mu
    var = jnp.mean(yc * yc, axis=-1, keepdims=True)
    z = yc * lax.rsqrt(var + EPS) * lng_ref[...] + lnb_ref[...]
    z = (z * jax.nn.sigmoid(z)).astype(_BF16)
    conv_unit(0, None)

    pw = _dot(z, wpw_ref[...])
    after_dot(0, pw)
    gates = gate_ref[...].astype(_F32)
    merged = gates[:, :d] * attn_ref[...] + gates[:, d:] * (pw + bpw_ref[...])
    out = _dot(merged.astype(_BF16), wout_ref[...])
    after_dot(1, out)
    h1 = x_ref[...] + out
    u = (_rms(h1) * g2_ref[...]).astype(_BF16)
    acc = jnp.zeros(h1.shape, _F32)
    for n, c0 in enumerate(range(0, dff, d)):
        up = _dot(u, wup_ref[:, c0:c0 + d])
        after_dot(2 + 2 * n, up)
        hid = jnp.maximum(up, 0.0)
        down = _dot((hid * hid).astype(_BF16), wdown_ref[c0:c0 + d, :])
        after_dot(3 + 2 * n, down)
        acc = acc + down
    h2 = h1 + acc
    if final_norm:
        h2 = _rms(h2) * gf_ref[...]
    o_ref[...] = h2


def _branch_mlp(x2, attn, gates, c, wb, ln_g, ln_b, wpw, bpw, wout, g2, wup, wdown, gf, *, seq, final_norm):
    n, d = x2.shape
    ch = c.shape[1]
    rows = MLP_ROWS
    nt = n // rows
    halo_per_tile = rows // CONV_HALO
    prev = lambda w: pl.BlockSpec((rows, w), lambda s: (jnp.maximum(s - 1, 0), 0))
    cur = lambda s: jnp.minimum(s, nt - 1)
    return pl.pallas_call(
        functools.partial(_branch_mlp_kernel, final_norm=final_norm, tiles_per_seq=seq // rows),
        grid=(nt + 1,),
        in_specs=[prev(d), prev(d), prev(2 * d),
                  pl.BlockSpec((rows, ch), lambda s: (cur(s), 0)),
                  pl.BlockSpec((CONV_HALO, ch), lambda s: (jnp.maximum(cur(s) * halo_per_tile - 1, 0), 0)),
                  _resident(wb.shape), _resident((SUBLANES, CONV_LANES)), _resident((1, ch)), _resident((1, ch)),
                  _resident(wpw.shape),
                  _resident((1, d)), _resident(wout.shape), _resident((1, d)), _resident(wup.shape),
                  _resident(wdown.shape), _resident((1, d))],
        out_specs=prev(d),
        out_shape=jax.ShapeDtypeStruct((n, d), _F32),
        scratch_shapes=[pltpu.VMEM((rows, ch), _F32),
                        pltpu.VMEM((CONV_SUB + CONV_HALO, ch), _F32),
                        pltpu.VMEM((SUBLANES - 1, CONV_SUB + CONV_HALO, ch), _F32)],
        compiler_params=_params("arbitrary"),
        name="branch_mlp",
    )(x2, attn, gates, c, c, wb, jnp.zeros((SUBLANES, CONV_LANES), jnp.uint32), ln_g, ln_b, wpw, bpw, wout, g2, wup,
      wdown, gf)


def kernel(x, rel_bias, final_norm_g, norm_mix_g, w_in, b_glu, b_gate, lam_q1, lam_k1, lam_q2, lam_k2,
           subln_g, conv_w, conv_b, conv_ln_g, conv_ln_b, w_pw2, b_pw2, w_out, norm_mlp_g, w_up, w_down):
    batch, seq, d = x.shape
    depth = w_in.shape[0]
    ch = conv_w.shape[2]
    d_qk = (w_in.shape[2] - 2 * ch - 2 * d) // 3
    c1, c2, c3 = 2 * d_qk, 3 * d_qk, 3 * d_qk + 2 * ch
    assert seq % ATTN_TILE == 0 and seq % PROJ_ROWS == 0 and PROJ_ROWS % ATTN_TILE == 0
    assert seq % MLP_ROWS == 0 and MLP_ROWS % CONV_SUB == 0 and ATTN_TILE >= MAX_DISTANCE

    row = lambda v: v.reshape(1, -1)
    h = x.reshape(batch * seq, d)
    bias = _bias_tiles(rel_bias)
    for l in range(depth):
        lambda_init = 0.8 - 0.6 * math.exp(-0.3 * l)
        w = w_in[l]
        q, k, vt, c, gates = _proj(
            h, row(norm_mix_g[l]), w[:, :c1].astype(_BF16), w[:, c1:c2].T.astype(_BF16),
            w[:, c2:c3].astype(_BF16), w[:, c3:].astype(_BF16), row(b_glu[l]), row(b_gate[l]),
            batch=batch, seq=seq)
        attn = _attention(q, k, vt, bias, row(lam_q1[l]), row(lam_k1[l]), row(lam_q2[l]), row(lam_k2[l]),
                          subln_g[l].reshape(-1, 1), batch=batch, seq=seq, lambda_init=lambda_init)
        wb = jnp.concatenate([conv_w[l], conv_b[l][None]], axis=0)
        wb = jnp.broadcast_to(wb[:, None, :], (CONV_WIDTH + 1, SUBLANES, ch))
        h = _branch_mlp(h, attn, gates, c, wb, row(conv_ln_g[l]), row(conv_ln_b[l]), w_pw2[l].astype(_BF16),
                        row(b_pw2[l]), w_out[l].astype(_BF16), row(norm_mlp_g[l]), w_up[l].astype(_BF16),
                        w_down[l].astype(_BF16), row(final_norm_g), seq=seq, final_norm=(l == depth - 1))
    return h.reshape(batch, seq, d)
```

```python
import functools
import math

import jax
import jax.numpy as jnp
import numpy as np
from jax import lax
from jax.experimental import pallas as pl
from jax.experimental.pallas import tpu as pltpu

HEAD_DIM = 64
V_HEAD_DIM = 2 * HEAD_DIM
CONV_WIDTH = 31
NUM_BUCKETS = 32
MAX_DISTANCE = 128
EPS = 1e-6
MASKED = -1e30
LOG2E = 1.4426950408889634

PROJ_ROWS = 512
ATTN_TILE = 512
ATTN_COLS = 256
ATTN_HEADS = 2
MLP_ROWS = 512
CONV_SUB = 128
CONV_HALO = 32
CONV_CHUNK = 64
CONV_LANES = 128
SUBLANES = 8
VMEM_LIMIT = 56 * 1024 * 1024

_BF16 = jnp.bfloat16
_F32 = jnp.float32


def _resident(shape):
    return pl.BlockSpec(shape, lambda *_: (0,) * len(shape), pipeline_mode=pl.Buffered(1))


def _params(*semantics):
    return pltpu.CompilerParams(dimension_semantics=semantics, vmem_limit_bytes=VMEM_LIMIT)


def _rms(xf):
    return xf * lax.rsqrt(jnp.mean(xf * xf, axis=-1, keepdims=True) + EPS)


def _dot(a, b):
    return jnp.dot(a, b, preferred_element_type=_F32)


def _dot_nt(a, b):
    return lax.dot_general(a, b, (((1,), (1,)), ((), ())), preferred_element_type=_F32)


def _proj_kernel(x_ref, g_ref, wqk_ref, wvt_ref, wglu_ref, wgate_ref, bglu_ref, bgate_ref,
                 q_ref, k_ref, vt_ref, c_ref, gate_ref, *, d_qk, kv_tiles_per_block):
    u = (_rms(x_ref[...]) * g_ref[...]).astype(_BF16)
    ch = c_ref.shape[-1]
    for g0 in range(0, gate_ref.shape[-1], ch):
        gate_ref[:, g0:g0 + ch] = jax.nn.sigmoid(
            _dot(u, wgate_ref[:, g0:g0 + ch]) + bgate_ref[:, g0:g0 + ch]).astype(_BF16)
    ga = _dot(u, wglu_ref[:, :ch]) + bglu_ref[:, :ch]
    gb = _dot(u, wglu_ref[:, ch:]) + bglu_ref[:, ch:]
    c_ref[...] = ga * jax.nn.sigmoid(gb)
    scale = LOG2E / math.sqrt(HEAD_DIM)
    q_ref[...] = (_dot(u, wqk_ref[:, :d_qk]) * scale).astype(_BF16)
    vt = _dot_nt(wvt_ref[...], u).astype(_BF16)
    for t in range(kv_tiles_per_block):
        vt_ref[0, t] = vt[:, t * ATTN_TILE:(t + 1) * ATTN_TILE]
    k_ref[...] = _dot(u, wqk_ref[:, d_qk:]).astype(_BF16)


def _proj(x2, g, wqk, wvt, wglu, wgate, bglu, bgate, *, batch, seq):
    n, d = x2.shape
    d_qk = wqk.shape[1] // 2
    aw = wvt.shape[0]
    ch = wglu.shape[1] // 2
    ngate = wgate.shape[1]
    rows = PROJ_ROWS
    tiles = rows // ATTN_TILE
    blocks_per_seq = seq // rows
    row_spec = lambda w: pl.BlockSpec((rows, w), lambda i: (i, 0))
    return pl.pallas_call(
        functools.partial(_proj_kernel, d_qk=d_qk, kv_tiles_per_block=tiles),
        grid=(n // rows,),
        in_specs=[row_spec(d), _resident((1, d)), _resident(wqk.shape), _resident(wvt.shape),
                  _resident(wglu.shape), _resident(wgate.shape), _resident(bglu.shape),
                  _resident(bgate.shape)],
        out_specs=[row_spec(d_qk), row_spec(d_qk),
                   pl.BlockSpec((1, tiles, aw, ATTN_TILE),
                                lambda i: (i // blocks_per_seq, i % blocks_per_seq, 0, 0)),
                   row_spec(ch), row_spec(ngate)],
        out_shape=[jax.ShapeDtypeStruct((n, d_qk), _BF16), jax.ShapeDtypeStruct((n, d_qk), _BF16),
                   jax.ShapeDtypeStruct((batch, seq // ATTN_TILE, aw, ATTN_TILE), _BF16),
                   jax.ShapeDtypeStruct((n, ch), _F32), jax.ShapeDtypeStruct((n, ngate), _BF16)],
        compiler_params=_params("parallel"),
        name="proj",
    )(x2, g, wqk, wvt, wglu, wgate, bglu, bgate)


def _bucket_table():
    t, w = ATTN_TILE, ATTN_COLS
    r = np.arange(t, dtype=np.int64)[:, None]
    c = np.arange(w, dtype=np.int64)[None, :]
    rel = (t - w) + c - r
    n = np.maximum(rel, 0)
    max_exact = NUM_BUCKETS // 2
    nf = np.maximum(n, 1).astype(np.float32)
    large = max_exact + (np.log(nf / np.float32(max_exact)) / np.float32(math.log(MAX_DISTANCE / max_exact))
                         * np.float32(NUM_BUCKETS - max_exact)).astype(np.int32)
    large = np.minimum(large, NUM_BUCKETS - 1)
    return np.where(rel >= 0, np.where(n < max_exact, n, large), -1).astype(np.int32)


def _bias_kernel(rb_ref, bucket_ref, out_ref):
    h = pl.program_id(0)
    bucket = bucket_ref[...]
    far = rb_ref[NUM_BUCKETS - 1, h]
    acc = jnp.full(bucket.shape, MASKED, _F32)
    for b in range(NUM_BUCKETS):
        acc = jnp.where(bucket == b, (rb_ref[b, h] - far) * LOG2E, acc)
    out_ref[0] = acc


def _bias_tiles(rel_bias):
    nh = rel_bias.shape[1]
    t, w = ATTN_TILE, ATTN_COLS
    return pl.pallas_call(
        _bias_kernel,
        grid=(nh,),
        in_specs=[pl.BlockSpec(memory_space=pltpu.SMEM), _resident((t, w))],
        out_specs=pl.BlockSpec((1, t, w), lambda h: (h, 0, 0)),
        out_shape=jax.ShapeDtypeStruct((nh, t, w), _F32),
        compiler_params=_params("parallel"),
        name="bias_tiles",
    )(rel_bias, jnp.asarray(_bucket_table()))


MAX_CHAINS = 8
_PLAIN, _DIAG = 0, 1
_SUM_ROWS = 16


def _attn_steps(nq):
    steps = [[i, j, _DIAG if j == i else _PLAIN, 0] for i in range(nq) for j in range(i + 1)]
    for n in range(1, len(steps)):
        steps[n][3] = int(steps[n - 1][2] == _DIAG)
    return np.asarray(steps, dtype=np.int32).T.copy()


def _col_max(s):
    groups = [s[r:r + SUBLANES] for r in range(0, s.shape[0], SUBLANES)]
    acc = groups[:MAX_CHAINS]
    for n, g in enumerate(groups[MAX_CHAINS:]):
        acc[n % MAX_CHAINS] = jnp.maximum(acc[n % MAX_CHAINS], g)
    while len(acc) > 1:
        acc = [jnp.maximum(acc[n], acc[n + 1]) for n in range(0, len(acc), 2)]
    return jnp.max(acc[0], axis=0, keepdims=True)


def _tile_rows(idx, offset=0, size=ATTN_TILE):
    start = idx * ATTN_TILE + offset
    if not isinstance(start, int):
        start = pl.multiple_of(start, math.gcd(ATTN_TILE, offset))
    return pl.ds(start, size)


def _attn_kernel(tab_ref, q_ref, k_ref, vt_ref, bias_ref, lq1_ref, lk1_ref, lq2_ref, lk2_ref, g_ref,
                 o_ref, s0_ref, s1_ref, mt_ref, m_ref, acc_ref, *, lambda_init, steps, heads):
    t, w = ATTN_TILE, ATTN_COLS
    s_ref = (s0_ref, s1_ref)
    nsteps = steps.shape[1]
    parts = [(hh, a, c) for hh in range(heads) for a in range(2) for c in range(0, t, w)]

    def head_cols(hh):
        return slice(hh * V_HEAD_DIM, (hh + 1) * V_HEAD_DIM)

    def scores(i, j, slot, kind, hh, a, c):
        cb = c // w
        q = q_ref[_tile_rows(i, c, w), head_cols(hh)]
        kk = k_ref[_tile_rows(j), head_cols(hh)]
        lane = lax.broadcasted_iota(jnp.int32, q.shape, 1)
        keep = lane < HEAD_DIM if a == 0 else lane >= HEAD_DIM
        s = _dot_nt(kk, jnp.where(keep, q, jnp.zeros_like(q)))
        if kind == _DIAG:
            bias = bias_ref[hh, t - w - c:, :]
            if c + w < t:
                bias = jnp.concatenate([bias, jnp.full((t - w - c, w), MASKED, _F32)], axis=0)
            s = s + bias
        elif c == 0:
            near = jnp.where(j == i - 1, 1.0, 0.0).astype(_F32)
            corner = s[t - MAX_DISTANCE:, :MAX_DISTANCE] + (
                bias_ref[hh, t - w - MAX_DISTANCE:t - w, :MAX_DISTANCE] * near)
            bottom = jnp.concatenate([corner, s[t - MAX_DISTANCE:, MAX_DISTANCE:]], axis=1)
            s = jnp.concatenate([s[:t - MAX_DISTANCE], bottom], axis=0)
        s_ref[slot][hh, a, cb] = s
        mt_ref[slot, hh, a, cb] = _col_max(s)

    def softmax_values(i, j, slot, hh, a, c):
        cb = c // w
        m_old = jnp.where(j == 0, -jnp.inf, m_ref[hh, a, cb])
        m_new = jnp.maximum(m_old, mt_ref[slot, hh, a, cb])
        alpha = jnp.exp2(m_old - m_new)
        p = jnp.exp2(s_ref[slot][hh, a, cb] - m_new).astype(_BF16)
        m_ref[hh, a, cb] = m_new
        vt = jnp.concatenate([vt_ref[0, j, head_cols(hh), :], jnp.ones((_SUM_ROWS, t), _BF16)], axis=0)
        acc_ref[i % 2, hh, a, cb] = alpha * acc_ref[i % 2, hh, a, cb] + _dot(vt, p)

    def finalize(i, hh, cb):
        lam = (jnp.exp(jnp.sum(lq1_ref[...] * lk1_ref[...])) - jnp.exp(jnp.sum(lq2_ref[...] * lk2_ref[...]))
               + lambda_init)
        a1 = acc_ref[i % 2, hh, 0, cb]
        a2 = acc_ref[i % 2, hh, 1, cb]
        r1 = 1.0 / a1[V_HEAD_DIM:V_HEAD_DIM + 1]
        r2 = lam / a2[V_HEAD_DIM:V_HEAD_DIM + 1]
        o = a1[:V_HEAD_DIM] * r1 - a2[:V_HEAD_DIM] * r2
        o = o * lax.rsqrt(jnp.mean(o * o, axis=0, keepdims=True) + EPS)
        o = o * (g_ref[...] * (1.0 - lambda_init))
        o_ref[_tile_rows(i, cb * w, w), head_cols(hh)] = o.T

    fin_pieces = [(hh, cb) for hh in range(heads) for cb in range(t // w)]

    def pipeline_step(sc, sv, fin):
        for hh, a, c in parts:
            if sc is not None:
                scores(sc[0], sc[1], sc[3], sc[2], hh, a, c)
            if sv is not None:
                softmax_values(sv[0], sv[1], sv[3], hh, a, c)
        if fin is not None:
            for hh, cb in fin_pieces:
                finalize(fin, hh, cb)

    def static_step(n):
        return int(steps[0, n]), int(steps[1, n]), int(steps[2, n]), n % 2

    def static_fin(n):
        return int(steps[0, n - 1]) if n >= 1 and steps[3, n] else None

    acc_ref[...] = jnp.zeros(acc_ref.shape, _F32)
    m_ref[...] = jnp.full(m_ref.shape, -jnp.inf, _F32)

    pipeline_step(static_step(0), None, None)

    full = nsteps - 1
    combos = sorted({(int(steps[2, n + 1]), int(steps[3, n])) for n in range(full)})

    def iteration(n, par):
        k_sc = tab_ref[2, n + 1]
        k_fin = tab_ref[3, n]
        for c_sc, c_fin in combos:
            @pl.when((k_sc == c_sc) & (k_fin == c_fin))
            def _():
                pipeline_step((tab_ref[0, n + 1], tab_ref[1, n + 1], c_sc, 1 - par),
                              (tab_ref[0, n], tab_ref[1, n], None, par),
                              tab_ref[0, jnp.maximum(n - 1, 0)] if c_fin else None)

    def body(pair, carry):
        iteration(2 * pair, 0)
        iteration(2 * pair + 1, 1)
        return carry

    lax.fori_loop(0, full // 2, body, 0)
    if full % 2:
        iteration(full - 1, (full - 1) % 2)

    pipeline_step(None, static_step(nsteps - 1), static_fin(nsteps - 1))
    pipeline_step(None, None, int(steps[0, nsteps - 1]))


def _attention(q, k, vt, bias, lq1, lk1, lq2, lk2, subln_g, *, batch, seq, lambda_init):
    n, d_qk = q.shape
    t = ATTN_TILE
    nq = seq // t
    nh = d_qk // V_HEAD_DIM
    hb = ATTN_HEADS
    aw = vt.shape[2]
    w, ncb = ATTN_COLS, t // ATTN_COLS
    steps = _attn_steps(nq)
    assert steps.shape[1] >= 3 and nh % hb == 0
    vec = _resident((1, HEAD_DIM))
    head_rows = pl.BlockSpec((seq, hb * V_HEAD_DIM), lambda b, h: (b, h))
    return pl.pallas_call(
        functools.partial(_attn_kernel, lambda_init=lambda_init, steps=steps, heads=hb),
        grid=(batch, nh // hb),
        in_specs=[pl.BlockSpec(memory_space=pltpu.SMEM), head_rows, head_rows,
                  pl.BlockSpec((1, nq, hb * V_HEAD_DIM, t), lambda b, h: (b, 0, h, 0)),
                  pl.BlockSpec((hb, t, w), lambda b, h: (h, 0, 0)),
                  vec, vec, vec, vec, _resident((V_HEAD_DIM, 1))],
        out_specs=head_rows,
        out_shape=jax.ShapeDtypeStruct((n, aw), _F32),
        scratch_shapes=[pltpu.VMEM((hb, 2, ncb, t, w), _F32), pltpu.VMEM((hb, 2, ncb, t, w), _F32),
                        pltpu.VMEM((2, hb, 2, ncb, 1, w), _F32), pltpu.VMEM((hb, 2, ncb, 1, w), _F32),
                        pltpu.VMEM((2, hb, 2, ncb, V_HEAD_DIM + _SUM_ROWS, w), _F32)],
        compiler_params=_params("parallel", "parallel"),
        name="diff_attn",
    )(jnp.asarray(steps), q, k, vt, bias, lq1, lk1, lq2, lk2, subln_g)


def _zero_after(value, zero_ref):
    bits = lax.bitcast_convert_type(value, jnp.uint32) & zero_ref[...]
    return lax.bitcast_convert_type(bits, _F32)


def _branch_mlp_kernel(x_ref, attn_ref, gate_ref, c_ref, halo_ref, wb_ref, zero_ref, lng_ref, lnb_ref,
                       wpw_ref, bpw_ref, wout_ref, g2_ref, wup_ref, wdown_ref, gf_ref, o_ref,
                       y_ref, pad_ref, sh_ref, *, final_norm, tiles_per_seq):
    s = pl.program_id(0)
    rows, d = x_ref.shape
    first_of_seq = s % tiles_per_seq == 0
    lead = CONV_HALO - (CONV_WIDTH - 1)
    span = CONV_SUB + CONV_HALO - SUBLANES
    groups = CONV_CHUNK // SUBLANES
    dff = wup_ref.shape[1]
    ndots = 2 + 2 * (dff // d)

    @pl.when(s == 0)
    def _():
        y_ref[...] = jnp.zeros(y_ref.shape, _F32)

    lane_blocks = y_ref.shape[1] // CONV_LANES
    nunits = (rows // CONV_SUB) * lane_blocks

    def conv_unit(q, zero):
        sub, lb = divmod(q, lane_blocks)
        lanes = slice(lb * CONV_LANES, (lb + 1) * CONV_LANES)
        base = sub * CONV_SUB
        if sub == 0:
            halo = halo_ref[:, lanes]
            pad_ref[:CONV_HALO, lanes] = jnp.where(first_of_seq, jnp.zeros_like(halo), halo)
        else:
            pad_ref[:CONV_HALO, lanes] = c_ref[base - CONV_HALO:base, lanes]
        body = c_ref[base:base + CONV_SUB, lanes]
        if zero is not None:
            body = (body.reshape(CONV_SUB // SUBLANES, SUBLANES, -1) + zero[None]).reshape(body.shape)
        pad_ref[CONV_HALO:, lanes] = body
        for b in range(1, SUBLANES):
            sh_ref[b - 1, :span, lanes] = pad_ref[b:b + span, lanes]
        bias = wb_ref[CONV_WIDTH, :, lanes]
        if zero is not None:
            bias = bias + zero
        taps = [wb_ref[kt, :, lanes] for kt in range(CONV_WIDTH)]
        for r0 in range(0, CONV_SUB, CONV_CHUNK):
            acc = [bias] * groups
            for b in range(SUBLANES):
                cls = [(divmod(kt + lead, SUBLANES)[0], kt) for kt in range(CONV_WIDTH)
                       if (kt + lead) % SUBLANES == b]
                for rho in range(min(a for a, _ in cls), groups + max(a for a, _ in cls)):
                    r = r0 + SUBLANES * rho
                    xs = pad_ref[r:r + SUBLANES, lanes] if b == 0 else sh_ref[b - 1, r:r + SUBLANES, lanes]
                    for a, kt in cls:
                        if 0 <= rho - a < groups:
                            acc[rho - a] = acc[rho - a] + xs * taps[kt]
            for g in range(groups):
                r = base + r0 + SUBLANES * g
                y_ref[r:r + SUBLANES, lanes] = acc[g]

    anchors = {}
    for q in range(1, nunits):
        pos = q * ndots / nunits
        dot_idx = int(pos)
        row = int((pos - dot_idx) * rows) // SUBLANES * SUBLANES
        anchors.setdefault(dot_idx, []).append((q, row))

    def after_dot(dot_idx, result):
        for q, row in anchors.get(dot_idx, []):
            lb = q % lane_blocks
            piece = result[row:row + SUBLANES, lb * CONV_LANES:(lb + 1) * CONV_LANES]
            conv_unit(q, _zero_after(piece, zero_ref))

    halves = []
    for r0 in range(0, rows, rows // 2):
        y = y_ref[r0:r0 + rows // 2]
        mu = jnp.mean(y, axis=-1, keepdims=True)
        yc = y - mu
        var = jnp.mean(yc * yc, axis=-1, keepdims=True)
        z = yc * lax.rsqrt(var + EPS) * lng_ref[...] + lnb_ref[...]
        halves.append(_dot((z * jax.nn.sigmoid(z)).astype(_BF16), wpw_ref[...]))
        if r0 == 0:
            conv_unit(0, None)
    pw = jnp.concatenate(halves, axis=0)
    after_dot(0, pw)
    gates = gate_ref[...].astype(_F32)
    merged = gates[:, :d] * attn_ref[...] + gates[:, d:] * (pw + bpw_ref[...])
    out = _dot(merged.astype(_BF16), wout_ref[...])
    after_dot(1, out)
    h1 = x_ref[...] + out
    u = (_rms(h1) * g2_ref[...]).astype(_BF16)
    acc = jnp.zeros(h1.shape, _F32)
    for n, c0 in enumerate(range(0, dff, d)):
        up = _dot(u, wup_ref[:, c0:c0 + d])
        after_dot(2 + 2 * n, up)
        hid = jnp.maximum(up, 0.0)
        down = _dot((hid * hid).astype(_BF16), wdown_ref[c0:c0 + d, :])
        after_dot(3 + 2 * n, down)
        acc = acc + down
    h2 = h1 + acc
    if final_norm:
        h2 = _rms(h2) * gf_ref[...]
    o_ref[...] = h2


def _branch_mlp(x2, attn, gates, c, wb, ln_g, ln_b, wpw, bpw, wout, g2, wup, wdown, gf, *, seq, final_norm):
    n, d = x2.shape
    ch = c.shape[1]
    rows = MLP_ROWS
    nt = n // rows
    halo_per_tile = rows // CONV_HALO
    prev = lambda w: pl.BlockSpec((rows, w), lambda s: (jnp.maximum(s - 1, 0), 0))
    cur = lambda s: jnp.minimum(s, nt - 1)
    return pl.pallas_call(
        functools.partial(_branch_mlp_kernel, final_norm=final_norm, tiles_per_seq=seq // rows),
        grid=(nt + 1,),
        in_specs=[prev(d), prev(d), prev(2 * d),
                  pl.BlockSpec((rows, ch), lambda s: (cur(s), 0)),
                  pl.BlockSpec((CONV_HALO, ch), lambda s: (jnp.maximum(cur(s) * halo_per_tile - 1, 0), 0)),
                  _resident(wb.shape), _resident((SUBLANES, CONV_LANES)), _resident((1, ch)), _resident((1, ch)),
                  _resident(wpw.shape),
                  _resident((1, d)), _resident(wout.shape), _resident((1, d)), _resident(wup.shape),
                  _resident(wdown.shape), _resident((1, d))],
        out_specs=prev(d),
        out_shape=jax.ShapeDtypeStruct((n, d), _F32),
        scratch_shapes=[pltpu.VMEM((rows, ch), _F32),
                        pltpu.VMEM((CONV_SUB + CONV_HALO, ch), _F32),
                        pltpu.VMEM((SUBLANES - 1, CONV_SUB + CONV_HALO, ch), _F32)],
        compiler_params=_params("arbitrary"),
        name="branch_mlp",
    )(x2, attn, gates, c, c, wb, jnp.zeros((SUBLANES, CONV_LANES), jnp.uint32), ln_g, ln_b, wpw, bpw, wout, g2, wup,
      wdown, gf)


def kernel(x, rel_bias, final_norm_g, norm_mix_g, w_in, b_glu, b_gate, lam_q1, lam_k1, lam_q2, lam_k2,
           subln_g, conv_w, conv_b, conv_ln_g, conv_ln_b, w_pw2, b_pw2, w_out, norm_mlp_g, w_up, w_down):
    batch, seq, d = x.shape
    depth = w_in.shape[0]
    ch = conv_w.shape[2]
    d_qk = (w_in.shape[2] - 2 * ch - 2 * d) // 3
    c1, c2, c3 = 2 * d_qk, 3 * d_qk, 3 * d_qk + 2 * ch
    assert seq % ATTN_TILE == 0 and seq % PROJ_ROWS == 0 and PROJ_ROWS % ATTN_TILE == 0
    assert seq % MLP_ROWS == 0 and MLP_ROWS % CONV_SUB == 0 and ATTN_TILE >= MAX_DISTANCE

    row = lambda v: v.reshape(1, -1)
    h = x.reshape(batch * seq, d)
    bias = _bias_tiles(rel_bias)
    for l in range(depth):
        lambda_init = 0.8 - 0.6 * math.exp(-0.3 * l)
        w = w_in[l]
        q, k, vt, c, gates = _proj(
            h, row(norm_mix_g[l]), w[:, :c1].astype(_BF16), w[:, c1:c2].T.astype(_BF16),
            w[:, c2:c3].astype(_BF16), w[:, c3:].astype(_BF16), row(b_glu[l]), row(b_gate[l]),
            batch=batch, seq=seq)
        attn = _attention(q, k, vt, bias, row(lam_q1[l]), row(lam_k1[l]), row(lam_q2[l]), row(lam_k2[l]),
                          subln_g[l].reshape(-1, 1), batch=batch, seq=seq, lambda_init=lambda_init)
        wb = jnp.concatenate([conv_w[l], conv_b[l][None]], axis=0)
        wb = jnp.broadcast_to(wb[:, None, :], (CONV_WIDTH + 1, SUBLANES, ch))
        h = _branch_mlp(h, attn, gates, c, wb, row(conv_ln_g[l]), row(conv_ln_b[l]), w_pw2[l].astype(_BF16),
                        row(b_pw2[l]), w_out[l].astype(_BF16), row(norm_mlp_g[l]), w_up[l].astype(_BF16),
                        w_down[l].astype(_BF16), row(final_norm_g), seq=seq, final_norm=(l == depth - 1))
    return h.reshape(batch, seq, d)
```
